```python
import math
import jax, jax.numpy as jnp
from jax import lax
import numpy as np

D_MODEL = 2048
BATCH = 2
SEQ = 16384
DEPTH = 1

CHUNK = 64
CONV_WIDTH = D_MODEL // 2
CONV_K = 3
SGU_WIDTH = D_MODEL // 2
SGU_BLOCK = 128
SGU_GROUP_CH = 128
N_SGU_GROUPS = SGU_WIDTH // SGU_GROUP_CH
IN_COLS = 3 * CONV_WIDTH + 2 * SGU_WIDTH + 2 * D_MODEL
N_EXPERTS = 32
TOP_K = 4
D_FF = D_MODEL
SWIGLU_ALPHA = 1.702
SWIGLU_LIMIT = 7.0
MOE_BLOCK = 512
LN_EPS = 1e-5
DEEPNORM_ALPHA = (2.0 * DEPTH) ** 0.25
DEEPNORM_BETA = (8.0 * DEPTH) ** -0.25

kernel_name = "hybrid_shortconv_sgu_moe_deepnorm"


def layer_norm(x, g, b):
    xf = x.astype(jnp.float32)
    mu = jnp.mean(xf, axis=-1, keepdims=True)
    var = jnp.mean(jnp.square(xf - mu), axis=-1, keepdims=True)
    y = (xf - mu) * lax.rsqrt(var + LN_EPS) * g.astype(jnp.float32) + b.astype(jnp.float32)
    return y.astype(x.dtype)


def short_conv_mixer(pre_gate, hidden, post_gate, conv_w, w_out):
    S = hidden.shape[1]
    z = pre_gate * hidden
    zp = jnp.pad(z, ((0, 0), (CONV_K - 1, 0), (0, 0)))
    conv = sum(conv_w[k] * zp[:, k:k + S] for k in range(CONV_K))
    return (post_gate * conv) @ w_out


def spatial_gating_mixer(zb, ln_g, ln_b, w_s, b_s, w_out):
    Bsz, S, _ = zb.shape
    z = jax.nn.gelu(zb, approximate=False)
    u, v = jnp.split(z, 2, axis=-1)
    v = layer_norm(v, ln_g, ln_b)
    v = v.reshape(Bsz, S // SGU_BLOCK, SGU_BLOCK, N_SGU_GROUPS, SGU_GROUP_CH)
    pos = jnp.arange(SGU_BLOCK)
    mask = (pos[None, :] // CHUNK) <= (pos[:, None] // CHUNK)
    w = jnp.where(mask[None], w_s, jnp.zeros_like(w_s))
    mixed = jnp.einsum('gpq,bnqgc->bnpgc', w, v) + b_s.T[None, None, :, :, None]
    mixed = mixed.reshape(Bsz, S, SGU_WIDTH)
    return (u * mixed) @ w_out


def clamped_swiglu(a):
    glu, lin = jnp.split(a, 2, axis=-1)
    glu = jnp.minimum(glu, SWIGLU_LIMIT)
    lin = jnp.clip(lin, -SWIGLU_LIMIT, SWIGLU_LIMIT)
    return glu * jax.nn.sigmoid(SWIGLU_ALPHA * glu) * (lin + 1.0)


def moe(h, w_router, b_router, w_up, b_up, w_down, b_down):
    Bsz, S, D = h.shape
    T = Bsz * S
    hf = h.reshape(T, D)
    logits = (hf @ w_router + b_router).astype(jnp.float32)
    top_vals, top_idx = lax.top_k(logits, TOP_K)
    gates = jax.nn.softmax(top_vals, axis=-1)
    n_assign = T * TOP_K
    e_flat = top_idx.reshape(-1)
    tok_flat = jnp.arange(n_assign, dtype=jnp.int32) // TOP_K
    g_flat = gates.reshape(-1)
    order = jnp.argsort(e_flat)
    e_sorted = e_flat[order]
    counts = jnp.bincount(e_flat, length=N_EXPERTS)
    padded = ((counts + MOE_BLOCK - 1) // MOE_BLOCK) * MOE_BLOCK
    starts = jnp.cumsum(counts) - counts
    pends = jnp.cumsum(padded)
    pstarts = pends - padded
    rank = jnp.arange(n_assign, dtype=jnp.int32) - starts[e_sorted]
    dest = pstarts[e_sorted] + rank
    n_blocks = -(-n_assign // MOE_BLOCK) + N_EXPERTS
    n_rows = n_blocks * MOE_BLOCK
    row_tok = jnp.full((n_rows,), T, dtype=jnp.int32).at[dest].set(tok_flat[order])
    row_gate = jnp.zeros((n_rows,), h.dtype).at[dest].set(g_flat[order].astype(h.dtype))
    block_start = jnp.arange(n_blocks, dtype=jnp.int32) * MOE_BLOCK
    block_expert = jnp.minimum(jnp.searchsorted(pends, block_start, side='right'), N_EXPERTS - 1)
    h_pad = jnp.concatenate([hf, jnp.zeros((1, D), hf.dtype)], axis=0)

    def body(y, blk):
        toks, gts, e = blk
        xb = h_pad[toks]
        a = xb @ w_up[e] + b_up[e]
        o = clamped_swiglu(a) @ w_down[e] + b_down[e]
        return y.at[toks].add(o * gts[:, None]), None

    y0 = jnp.zeros((T + 1, D), h.dtype)
    y, _ = lax.scan(body, y0, (row_tok.reshape(n_blocks, MOE_BLOCK),
                               row_gate.reshape(n_blocks, MOE_BLOCK),
                               block_expert))
    return y[:T].reshape(Bsz, S, D)


def setup_inputs(seed: int = 0) -> dict:
    key = jax.random.key(seed)
    ks = jax.random.split(key, 24)
    f32 = jnp.float32
    L = DEPTH

    def nrm(k, shape, scale):
        return jax.random.normal(k, shape, f32) * scale

    return {
        "x": nrm(ks[0], (BATCH, SEQ, D_MODEL), 1.0),
        "w_in": nrm(ks[1], (L, D_MODEL, IN_COLS), D_MODEL ** -0.5),
        "conv_w": nrm(ks[2], (L, CONV_K, CONV_WIDTH), CONV_K ** -0.5),
        "w_a_out": nrm(ks[3], (L, CONV_WIDTH, D_MODEL), CONV_WIDTH ** -0.5 * DEEPNORM_BETA),
        "ln_v_g": 1.0 + nrm(ks[4], (L, SGU_WIDTH), 0.01),
        "ln_v_b": nrm(ks[5], (L, SGU_WIDTH), 0.01),
        "w_s": nrm(ks[6], (L, N_SGU_GROUPS, SGU_BLOCK, SGU_BLOCK), SGU_BLOCK ** -0.5),
        "b_s": 1.0 + nrm(ks[7], (L, N_SGU_GROUPS, SGU_BLOCK), 0.01),
        "w_b_out": nrm(ks[8], (L, SGU_WIDTH, D_MODEL), SGU_WIDTH ** -0.5 * DEEPNORM_BETA),
        "b_gate": nrm(ks[9], (L, 2 * D_MODEL), 0.01),
        "w_o": nrm(ks[10], (L, D_MODEL, D_MODEL), D_MODEL ** -0.5 * DEEPNORM_BETA),
        "ln1_g": 1.0 + nrm(ks[11], (L, D_MODEL), 0.01),
        "ln1_b": nrm(ks[12], (L, D_MODEL), 0.01),
        "w_router": nrm(ks[13], (L, D_MODEL, N_EXPERTS), D_MODEL ** -0.5),
        "b_router": nrm(ks[14], (L, N_EXPERTS), 0.01),
        "w_up": nrm(ks[15], (L, N_EXPERTS, D_MODEL, 2 * D_FF), D_MODEL ** -0.5 * DEEPNORM_BETA),
        "b_up": nrm(ks[16], (L, N_EXPERTS, 2 * D_FF), 0.01),
        "w_down": nrm(ks[17], (L, N_EXPERTS, D_FF, D_MODEL), D_FF ** -0.5 * DEEPNORM_BETA),
        "b_down": nrm(ks[18], (L, N_EXPERTS, D_MODEL), 0.01),
        "ln2_g": 1.0 + nrm(ks[19], (L, D_MODEL), 0.01),
        "ln2_b": nrm(ks[20], (L, D_MODEL), 0.01),
    }


def reference(x, w_in, conv_w, w_a_out, ln_v_g, ln_v_b, w_s, b_s, w_b_out, b_gate,
              w_o, ln1_g, ln1_b, w_router, b_router, w_up, b_up, w_down, b_down,
              ln2_g, ln2_b):
    c0 = CONV_WIDTH
    c1 = 2 * CONV_WIDTH
    c2 = 3 * CONV_WIDTH
    c3 = c2 + 2 * SGU_WIDTH
    for l in range(DEPTH):
        p = x @ w_in[l]
        pre_gate, hidden, post_gate = p[..., :c0], p[..., c0:c1], p[..., c1:c2]
        zb = p[..., c2:c3]
        g = jax.nn.sigmoid(p[..., c3:] + b_gate[l])
        g_a, g_b = g[..., :D_MODEL], g[..., D_MODEL:]
        y_a = short_conv_mixer(pre_gate, hidden, post_gate, conv_w[l], w_a_out[l])
        y_b = spatial_gating_mixer(zb, ln_v_g[l], ln_v_b[l], w_s[l], b_s[l], w_b_out[l])
        mix = (g_a * y_a + g_b * y_b) @ w_o[l]
        x = layer_norm(DEEPNORM_ALPHA * x + mix, ln1_g[l], ln1_b[l])
        f = moe(x, w_router[l], b_router[l], w_up[l], b_up[l], w_down[l], b_down[l])
        x = layer_norm(DEEPNORM_ALPHA * x + f, ln2_g[l], ln2_b[l])
    return x
```

```python
import functools

import jax
import jax.numpy as jnp
from jax import lax
from jax.experimental import pallas as pl
from jax.experimental.pallas import tpu as pltpu

CHUNK = 64
CONV_K = 3
SGU_BLOCK = 128
SGU_GROUP_CH = 128
TOP_K = 4
SWIGLU_ALPHA = 1.702
SWIGLU_LIMIT = 7.0
LN_EPS = 1e-5

LANES = 128
SUBLANES = 8
VMEM_BYTES_V7X = 64 * 1024 * 1024

INPROJ_ROWS = 1024
INPROJ_COLS = 1024
MIXER_ROWS = 256
MOE_ROWS = 512
MOE_FF_COLS = 512
COMBINE_ROWS = 256

_F32 = jnp.float32
_BF16 = jnp.bfloat16


def _tile(n, pref):
    if n <= pref:
        return n
    for c in range(pref - pref % LANES, 0, -LANES):
        if n % c == 0:
            return c
    raise ValueError(f"no lane-aligned tile of {n} within {pref}")


def _vmem_limit(est_bytes):
    return int(min(VMEM_BYTES_V7X - 4 * 1024 * 1024, est_bytes))


def _layer_norm(h, g, b):
    mu = jnp.mean(h, axis=-1, keepdims=True)
    c = h - mu
    var = jnp.mean(c * c, axis=-1, keepdims=True)
    return c * lax.rsqrt(var + LN_EPS) * g + b


def _gelu(x):
    return 0.5 * x * (1.0 + lax.erf(x * (2.0 ** -0.5)))


def _inproj_kernel(x_ref, w_ref, o_ref, xb_ref):
    @pl.when(pl.program_id(1) == 0)
    def _():
        xb_ref[...] = x_ref[...].astype(_BF16)

    o_ref[...] = jnp.dot(xb_ref[...], w_ref[...], preferred_element_type=_F32).astype(o_ref.dtype)


def _inproj(x2d, w_in):
    t, d = x2d.shape
    nc = w_in.shape[1]
    tm = _tile(t, INPROJ_ROWS)
    tn = _tile(nc, INPROJ_COLS)
    est = 2 * tm * d * 4 + tm * d * 2 + 2 * d * tn * 2 + 2 * tm * tn * 2 + 2 * tm * tn * 4
    return pl.pallas_call(
        _inproj_kernel,
        grid=(t // tm, nc // tn),
        in_specs=[
            pl.BlockSpec((tm, d), lambda i, j: (i, 0)),
            pl.BlockSpec((d, tn), lambda i, j: (0, j)),
        ],
        out_specs=pl.BlockSpec((tm, tn), lambda i, j: (i, j)),
        out_shape=jax.ShapeDtypeStruct((t, nc), _BF16),
        scratch_shapes=[pltpu.VMEM((tm, d), _BF16)],
        compiler_params=pltpu.CompilerParams(
            dimension_semantics=("arbitrary", "arbitrary"),
            vmem_limit_bytes=_vmem_limit(est + 8 * 1024 * 1024),
        ),
        name="inproj",
    )(x2d, w_in)


def _mixer_kernel(p_ref, x_ref, cw_ref, wa_ref, lng_ref, lnb_ref, ws_ref, bst_ref, wb_ref,
                  bg_ref, wo_ref, l1g_ref, l1b_ref, wr_ref, br_ref,
                  x1_ref, ti_ref, tg_ref,
                  zc_ref, v_ref, gb_ref, *, tiles_per_seq, alpha, n_exp):
    i = pl.program_id(0)
    tm, d = x_ref.shape
    cw = cw_ref.shape[1]
    sw = lng_ref.shape[1]
    n_groups = sw // SGU_GROUP_CH
    n_blk = tm // SGU_BLOCK

    z = p_ref[:, 0:cw].astype(_F32) * p_ref[:, cw:2 * cw].astype(_F32)

    @pl.when(i % tiles_per_seq == 0)
    def _():
        zc_ref[...] = jnp.zeros_like(zc_ref)

    prev1 = zc_ref[7:8, :]
    prev2 = zc_ref[6:7, :]
    row = lax.broadcasted_iota(jnp.int32, (tm, cw), 0)
    z1 = jnp.where(row == 0, prev1, pltpu.roll(z, 1, 0))
    z2 = jnp.where(row == 0, prev2, jnp.where(row == 1, prev1, pltpu.roll(z, 2, 0)))
    zc_ref[...] = z[tm - 8:tm, :]
    conv = cw_ref[0:1, :] * z2 + cw_ref[1:2, :] * z1 + cw_ref[2:3, :] * z
    post = p_ref[:, 2 * cw:3 * cw].astype(_F32)
    ya = jnp.dot((post * conv).astype(_BF16), wa_ref[...], preferred_element_type=_F32)

    c2 = 3 * cw
    v = _gelu(p_ref[:, c2 + sw:c2 + 2 * sw].astype(_F32))
    v_ref[...] = _layer_norm(v, lng_ref[...], lnb_ref[...]).astype(_BF16)
    pos_p = lax.broadcasted_iota(jnp.int32, (SGU_BLOCK, SGU_BLOCK), 0)
    pos_q = lax.broadcasted_iota(jnp.int32, (SGU_BLOCK, SGU_BLOCK), 1)
    causal = (pos_q // CHUNK) <= (pos_p // CHUNK)
    for g in range(n_groups):
        cs = slice(g * SGU_GROUP_CH, (g + 1) * SGU_GROUP_CH)
        wm = jnp.where(causal, ws_ref[g], 0.0).astype(_BF16)
        vcat = jnp.concatenate(
            [v_ref[n * SGU_BLOCK:(n + 1) * SGU_BLOCK, cs] for n in range(n_blk)], axis=1)
        mixed = jnp.dot(wm, vcat, preferred_element_type=_F32) + bst_ref[:, g:g + 1]
        for n in range(n_blk):
            rs = slice(n * SGU_BLOCK, (n + 1) * SGU_BLOCK)
            u = _gelu(p_ref[rs, c2 + g * SGU_GROUP_CH:c2 + (g + 1) * SGU_GROUP_CH].astype(_F32))
            gb_ref[rs, cs] = (u * mixed[:, n * SGU_BLOCK:(n + 1) * SGU_BLOCK]).astype(_BF16)
    yb = jnp.dot(gb_ref[...], wb_ref[...], preferred_element_type=_F32)

    c3 = c2 + 2 * sw
    ga = jax.nn.sigmoid(p_ref[:, c3:c3 + d].astype(_F32) + bg_ref[:, 0:d])
    gbr = jax.nn.sigmoid(p_ref[:, c3 + d:c3 + 2 * d].astype(_F32) + bg_ref[:, d:2 * d])
    m = (ga * ya + gbr * yb).astype(_BF16)
    mix = jnp.dot(m, wo_ref[...], preferred_element_type=_F32)
    x1 = _layer_norm(alpha * x_ref[...] + mix, l1g_ref[...], l1b_ref[...])
    x1_ref[...] = x1

    logits = jnp.dot(x1.astype(_BF16), wr_ref[...], preferred_element_type=_F32) + br_ref[...]
    lane = lax.broadcasted_iota(jnp.int32, logits.shape, 1).astype(_F32)
    neg_inf = jnp.float32(-jnp.inf)
    vals = jnp.where(lane < n_exp, logits, neg_inf)
    top_i = jnp.zeros(logits.shape, _F32)
    top_e = jnp.zeros(logits.shape, _F32)
    v_max = None
    for k in range(TOP_K):
        vk = jnp.max(vals, axis=1, keepdims=True)
        ik = jnp.min(jnp.where(vals == vk, lane, float(LANES)), axis=1, keepdims=True)
        vals = jnp.where(lane == ik, neg_inf, vals)
        if k == 0:
            v_max = vk
        top_i = jnp.where(lane == k, ik, top_i)
        top_e = jnp.where(lane == k, jnp.exp(vk - v_max), top_e)
    ti_ref[...] = top_i.astype(jnp.int32)
    tg_ref[...] = top_e / jnp.sum(top_e, axis=1, keepdims=True)


def _mixer(p, x2d, conv_w, w_a, ln_g, ln_b, w_s, b_s_t, w_b, b_gate, w_o, l1g, l1b,
           w_r, b_r, *, seq, alpha, n_exp):
    t, d = x2d.shape
    nc = p.shape[1]
    cw = conv_w.shape[1]
    sw = ln_g.shape[1]
    tm = _tile(seq, MIXER_ROWS)
    assert tm % SGU_BLOCK == 0 and tm >= 8

    def const(shape):
        nd = len(shape)
        return pl.BlockSpec(shape, lambda i: (0,) * nd, pipeline_mode=pl.Buffered(1))

    weights_bytes = (cw * d + sw * d + d * d + d * LANES) * 2
    est = (2 * tm * nc * 2 + 4 * tm * d * 4 + weights_bytes + 2 * tm * sw * 2
           + 4 * tm * LANES * 4)
    return pl.pallas_call(
        functools.partial(_mixer_kernel, tiles_per_seq=seq // tm, alpha=alpha, n_exp=n_exp),
        grid=(t // tm,),
        in_specs=[
            pl.BlockSpec((tm, nc), lambda i: (i, 0)),
            pl.BlockSpec((tm, d), lambda i: (i, 0)),
            const(conv_w.shape), const(w_a.shape), const(ln_g.shape), const(ln_b.shape),
            const(w_s.shape), const(b_s_t.shape), const(w_b.shape), const(b_gate.shape),
            const(w_o.shape), const(l1g.shape), const(l1b.shape), const(w_r.shape),
            const(b_r.shape),
        ],
        out_specs=[
            pl.BlockSpec((tm, d), lambda i: (i, 0)),
            pl.BlockSpec((tm, LANES), lambda i: (i, 0)),
            pl.BlockSpec((tm, LANES), lambda i: (i, 0)),
        ],
        out_shape=[
            jax.ShapeDtypeStruct((t, d), _F32),
            jax.ShapeDtypeStruct((t, LANES), jnp.int32),
            jax.ShapeDtypeStruct((t, LANES), _F32),
        ],
        scratch_shapes=[
            pltpu.VMEM((8, cw), _F32),
            pltpu.VMEM((tm, sw), _BF16),
            pltpu.VMEM((tm, sw), _BF16),
        ],
        compiler_params=pltpu.CompilerParams(
            dimension_semantics=("arbitrary",),
            vmem_limit_bytes=_vmem_limit(est + 24 * 1024 * 1024),
        ),
        name="mixer",
    )(p, x2d, conv_w, w_a, ln_g, ln_b, w_s, b_s_t, w_b, b_gate, w_o, l1g, l1b, w_r, b_r)


def _moe_kernel(be_ref, bv_ref,
                tokc_ref, tokn_ref, dst_ref, x_hbm, wg_ref, wl_ref, bug_ref, bul_ref,
                wd_ref, bd_ref,
                y_hbm,
                xbuf, xb16, acc_ref, obuf, gsem, ssem):
    del be_ref
    i = pl.program_id(0)
    j = pl.program_id(1)
    n_blocks = pl.num_programs(0)
    n_j = pl.num_programs(1)
    slot = i % 2
    nv = bv_ref[i]

    def gather_row(tok_ref, r, s):
        return pltpu.make_async_copy(
            x_hbm.at[pl.ds(tok_ref[0, r], 1)], xbuf.at[s, pl.ds(r, 1)], gsem.at[s])

    def sublane_ceil(n):
        return ((n + SUBLANES - 1) // SUBLANES) * SUBLANES

    def start_gather(tok_ref, n, s):
        def body(r, c):
            gather_row(tok_ref, r, s).start()
            return c
        lax.fori_loop(0, sublane_ceil(n), body, 0)

    def wait_gather(n, s):
        n8 = pl.multiple_of(sublane_ceil(n), SUBLANES)
        pltpu.make_async_copy(x_hbm.at[pl.ds(0, n8)], xbuf.at[s, pl.ds(0, n8)], gsem.at[s]).wait()

    def scatter_row(r, s):
        return pltpu.make_async_copy(
            obuf.at[s, pl.ds(r, 1)], y_hbm.at[pl.ds(dst_ref[0, r], 1)], ssem.at[s])

    def start_scatter(n, s):
        def body(r, c):
            scatter_row(r, s).start()
            return c
        lax.fori_loop(0, n, body, 0)

    def wait_scatter(n, s):
        n8 = pl.multiple_of((n // SUBLANES) * SUBLANES, SUBLANES)

        @pl.when(n8 > 0)
        def _():
            pltpu.make_async_copy(
                obuf.at[s, pl.ds(0, n8)], y_hbm.at[pl.ds(0, n8)], ssem.at[s]).wait()

        def body(r, c):
            pltpu.make_async_copy(
                obuf.at[s, pl.ds(r, 1)], y_hbm.at[pl.ds(0, 1)], ssem.at[s]).wait()
            return c
        lax.fori_loop(n8, n, body, 0)

    @pl.when(j == 0)
    def _():
        @pl.when(i == 0)
        def _():
            xbuf[...] = jnp.zeros_like(xbuf)
            start_gather(tokc_ref, nv, 0)

        @pl.when(i + 1 < n_blocks)
        def _():
            start_gather(tokn_ref, bv_ref[i + 1], 1 - slot)

        @pl.when(nv > 0)
        def _():
            wait_gather(nv, slot)
            xb16[...] = xbuf[slot].astype(_BF16)

    @pl.when(nv > 0)
    def _():
        xb = xb16[...]
        a_glu = jnp.dot(xb, wg_ref[...], preferred_element_type=_F32) + bug_ref[...]
        a_lin = jnp.dot(xb, wl_ref[...], preferred_element_type=_F32) + bul_ref[...]
        glu = jnp.minimum(a_glu, SWIGLU_LIMIT)
        lin = jnp.clip(a_lin, -SWIGLU_LIMIT, SWIGLU_LIMIT)
        h = glu * jax.nn.sigmoid(SWIGLU_ALPHA * glu) * (lin + 1.0)
        part = jnp.dot(h.astype(_BF16), wd_ref[...], preferred_element_type=_F32)

        @pl.when(j == 0)
        def _():
            acc_ref[...] = part

        @pl.when(jnp.logical_and(j > 0, j < n_j - 1))
        def _():
            acc_ref[...] += part

        @pl.when(j == n_j - 1)
        def _():
            @pl.when(i >= 2)
            def _():
                wait_scatter(bv_ref[jnp.maximum(i - 2, 0)], slot)
            obuf[slot] = acc_ref[...] + part + bd_ref[...]
            start_scatter(nv, slot)

    @pl.when(jnp.logical_and(nv == 0, jnp.logical_and(j == n_j - 1, i >= 2)))
    def _():
        wait_scatter(bv_ref[jnp.maximum(i - 2, 0)], slot)

    @pl.when(jnp.logical_and(i == n_blocks - 1, j == n_j - 1))
    def _():
        @pl.when(i >= 1)
        def _():
            wait_scatter(bv_ref[jnp.maximum(i - 1, 0)], 1 - slot)
        wait_scatter(nv, slot)


def _moe(x1, w_up, b_up, w_down, b_down, blk_expert, blk_valid, row_tok, row_dst):
    t, d = x1.shape
    n_exp, _, f2 = w_up.shape
    f = f2 // 2
    n_blocks = blk_expert.shape[0]
    tmb = row_dst.shape[-1]
    fc = _tile(f, MOE_FF_COLS)
    n_j = f // fc
    assert n_j >= 2

    def jj(i, j, bv):
        return jnp.where(bv[i] > 0, j, n_j - 1)

    smem = functools.partial(pl.BlockSpec, memory_space=pltpu.SMEM)
    est = (2 * tmb * d * 4 + tmb * d * 2 + tmb * d * 4 + 2 * tmb * d * 4
           + 2 * (2 * d * fc + fc * d) * 2 + 4 * tmb * fc * 4 + 2 * tmb * d * 4)
    return pl.pallas_call(
        _moe_kernel,
        grid_spec=pltpu.PrefetchScalarGridSpec(
            num_scalar_prefetch=2,
            grid=(n_blocks, n_j),
            in_specs=[
                smem((None, 1, tmb), lambda i, j, be, bv: (i, 0, 0)),
                smem((None, 1, tmb), lambda i, j, be, bv: (i + 1, 0, 0)),
                smem((None, 1, tmb), lambda i, j, be, bv: (i, 0, 0)),
                pl.BlockSpec(memory_space=pl.ANY),
                pl.BlockSpec((None, d, fc), lambda i, j, be, bv: (be[i], 0, jj(i, j, bv))),
                pl.BlockSpec((None, d, fc), lambda i, j, be, bv: (be[i], 0, n_j + jj(i, j, bv))),
                pl.BlockSpec((None, 1, fc), lambda i, j, be, bv: (be[i], 0, jj(i, j, bv))),
                pl.BlockSpec((None, 1, fc), lambda i, j, be, bv: (be[i], 0, n_j + jj(i, j, bv))),
                pl.BlockSpec((None, fc, d), lambda i, j, be, bv: (be[i], jj(i, j, bv), 0)),
                pl.BlockSpec((None, 1, d), lambda i, j, be, bv: (be[i], 0, 0)),
            ],
            out_specs=pl.BlockSpec(memory_space=pl.ANY),
            scratch_shapes=[
                pltpu.VMEM((2, tmb, d), _F32),
                pltpu.VMEM((tmb, d), _BF16),
                pltpu.VMEM((tmb, d), _F32),
                pltpu.VMEM((2, tmb, d), _F32),
                pltpu.SemaphoreType.DMA((2,)),
                pltpu.SemaphoreType.DMA((2,)),
            ],
        ),
        out_shape=jax.ShapeDtypeStruct((TOP_K * t, d), _F32),
        compiler_params=pltpu.CompilerParams(
            dimension_semantics=("arbitrary", "arbitrary"),
            vmem_limit_bytes=_vmem_limit(est + 8 * 1024 * 1024),
        ),
        name="moe",
    )(blk_expert, blk_valid, row_tok, row_tok, row_dst, x1, w_up, w_up,
      b_up.reshape(n_exp, 1, f2), b_up.reshape(n_exp, 1, f2), w_down,
      b_down.reshape(n_exp, 1, d))


def _routing_tables(top_idx, n_exp, tmb):
    t = top_idx.shape[0]
    n_assign = t * TOP_K
    e_flat = top_idx.reshape(-1)
    order = jnp.argsort(e_flat).astype(jnp.int32)
    counts = jnp.bincount(e_flat, length=n_exp).astype(jnp.int32)
    padded = ((counts + tmb - 1) // tmb) * tmb
    starts = jnp.cumsum(counts) - counts
    pends = jnp.cumsum(padded)
    pstarts = pends - padded
    n_blocks = -(-n_assign // tmb) + n_exp
    blk_start = jnp.arange(n_blocks, dtype=jnp.int32) * tmb
    blk_expert = jnp.minimum(
        jnp.searchsorted(pends, blk_start, side="right"), n_exp - 1).astype(jnp.int32)
    blk_valid = jnp.clip(counts[blk_expert] - (blk_start - pstarts[blk_expert]), 0, tmb)
    blk_valid = blk_valid.astype(jnp.int32)
    rows = jnp.arange(n_blocks * tmb, dtype=jnp.int32)
    row_e = jnp.repeat(blk_expert, tmb)
    rank = rows - pstarts[row_e]
    valid = rank < counts[row_e]
    assign = order[jnp.clip(starts[row_e] + rank, 0, n_assign - 1)]
    tok = assign // TOP_K
    slot_k = assign % TOP_K
    row_tok = jnp.where(valid, tok, 0).astype(jnp.int32)
    row_dst = jnp.where(valid, slot_k * t + tok, 0).astype(jnp.int32)
    row_tok = jnp.concatenate([row_tok, jnp.zeros((tmb,), jnp.int32)])
    return (blk_expert, blk_valid, row_tok.reshape(n_blocks + 1, 1, tmb),
            row_dst.reshape(n_blocks, 1, tmb))


def _combine_kernel(y_ref, x1_ref, g_ref, l2g_ref, l2b_ref, o_ref, *, alpha):
    y = g_ref[:, 0:1] * y_ref[0]
    for k in range(1, TOP_K):
        y = y + g_ref[:, k:k + 1] * y_ref[k]
    o_ref[...] = _layer_norm(alpha * x1_ref[...] + y, l2g_ref[...], l2b_ref[...])


def _combine(y_slots, x1, gates, l2g, l2b, *, alpha):
    t, d = x1.shape
    tm = _tile(t, COMBINE_ROWS)
    est = 2 * (TOP_K + 2) * tm * d * 4 + 2 * tm * LANES * 4 + 4 * tm * d * 4
    return pl.pallas_call(
        functools.partial(_combine_kernel, alpha=alpha),
        grid=(t // tm,),
        in_specs=[
            pl.BlockSpec((TOP_K, tm, d), lambda i: (0, i, 0)),
            pl.BlockSpec((tm, d), lambda i: (i, 0)),
            pl.BlockSpec((tm, LANES), lambda i: (i, 0)),
            pl.BlockSpec((1, d), lambda i: (0, 0)),
            pl.BlockSpec((1, d), lambda i: (0, 0)),
        ],
        out_specs=pl.BlockSpec((tm, d), lambda i: (i, 0)),
        out_shape=jax.ShapeDtypeStruct((t, d), _F32),
        compiler_params=pltpu.CompilerParams(
            dimension_semantics=("arbitrary",),
            vmem_limit_bytes=_vmem_limit(est + 8 * 1024 * 1024),
        ),
        name="combine",
    )(y_slots, x1, gates, l2g, l2b)


def kernel(x, w_in, conv_w, w_a_out, ln_v_g, ln_v_b, w_s, b_s, w_b_out, b_gate, w_o, ln1_g,
           ln1_b, w_router, b_router, w_up, b_up, w_down, b_down, ln2_g, ln2_b):
    bsz, seq, d = x.shape
    depth = w_in.shape[0]
    n_exp = w_router.shape[-1]
    alpha = (2.0 * depth) ** 0.25
    t = bsz * seq
    assert n_exp <= LANES and t * TOP_K % MOE_ROWS == 0
    h = x.reshape(t, d)
    for l in range(depth):
        p = _inproj(h, w_in[l].astype(_BF16))
        w_r = jnp.pad(w_router[l], ((0, 0), (0, LANES - n_exp))).astype(_BF16)
        b_r = jnp.pad(b_router[l], (0, LANES - n_exp)).reshape(1, LANES)
        x1, top_i, top_g = _mixer(
            p, h, conv_w[l], w_a_out[l].astype(_BF16), ln_v_g[l].reshape(1, -1),
            ln_v_b[l].reshape(1, -1), w_s[l], b_s[l].T, w_b_out[l].astype(_BF16),
            b_gate[l].reshape(1, -1), w_o[l].astype(_BF16), ln1_g[l].reshape(1, -1),
            ln1_b[l].reshape(1, -1), w_r, b_r, seq=seq, alpha=alpha, n_exp=n_exp)
        tables = _routing_tables(top_i[:, :TOP_K], n_exp, MOE_ROWS)
        y_rows = _moe(x1, w_up[l].astype(_BF16), b_up[l], w_down[l].astype(_BF16), b_down[l],
                      *tables)
        h = _combine(y_rows.reshape(TOP_K, t, d), x1, top_g, ln2_g[l].reshape(1, -1),
                     ln2_b[l].reshape(1, -1), alpha=alpha)
    return h.reshape(bsz, seq, d)
```

```python
import functools

import jax
import jax.numpy as jnp
from jax import lax
from jax.experimental import pallas as pl
from jax.experimental.pallas import tpu as pltpu

CHUNK = 64
CONV_K = 3
SGU_BLOCK = 128
SGU_GROUP_CH = 128
TOP_K = 4
SWIGLU_ALPHA = 1.702
SWIGLU_LIMIT = 7.0
LN_EPS = 1e-5

LANES = 128
SUBLANES = 8
VMEM_BYTES_V7X = 64 * 1024 * 1024

INPROJ_ROWS = 1024
INPROJ_COLS = 1024
MIXER_ROWS = 256
MOE_ROWS = 512
MOE_FF_COLS = 512
COMBINE_ROWS = 256

_F32 = jnp.float32
_BF16 = jnp.bfloat16


def _tile(n, pref):
    if n <= pref:
        return n
    for c in range(pref - pref % LANES, 0, -LANES):
        if n % c == 0:
            return c
    raise ValueError(f"no lane-aligned tile of {n} within {pref}")


def _vmem_limit(est_bytes):
    return int(min(VMEM_BYTES_V7X - 4 * 1024 * 1024, est_bytes))


def _layer_norm(h, g, b):
    mu = jnp.mean(h, axis=-1, keepdims=True)
    c = h - mu
    var = jnp.mean(c * c, axis=-1, keepdims=True)
    return c * lax.rsqrt(var + LN_EPS) * g + b


def _gelu(x):
    return 0.5 * x * (1.0 + lax.erf(x * (2.0 ** -0.5)))


def _inproj_kernel(x_ref, w_ref, o_ref, xb_ref):
    @pl.when(pl.program_id(1) == 0)
    def _():
        xb_ref[...] = x_ref[...].astype(_BF16)

    o_ref[...] = jnp.dot(xb_ref[...], w_ref[...], preferred_element_type=_F32).astype(o_ref.dtype)


def _inproj(x2d, w_in):
    t, d = x2d.shape
    nc = w_in.shape[1]
    tm = _tile(t, INPROJ_ROWS)
    tn = _tile(nc, INPROJ_COLS)
    est = 2 * tm * d * 4 + tm * d * 2 + 2 * d * tn * 2 + 2 * tm * tn * 2 + 2 * tm * tn * 4
    return pl.pallas_call(
        _inproj_kernel,
        grid=(t // tm, nc // tn),
        in_specs=[
            pl.BlockSpec((tm, d), lambda i, j: (i, 0)),
            pl.BlockSpec((d, tn), lambda i, j: (0, j)),
        ],
        out_specs=pl.BlockSpec((tm, tn), lambda i, j: (i, j)),
        out_shape=jax.ShapeDtypeStruct((t, nc), _BF16),
        scratch_shapes=[pltpu.VMEM((tm, d), _BF16)],
        compiler_params=pltpu.CompilerParams(
            dimension_semantics=("arbitrary", "arbitrary"),
            vmem_limit_bytes=_vmem_limit(est + 8 * 1024 * 1024),
        ),
        name="inproj",
    )(x2d, w_in)


def _mixer_kernel(p_ref, x_ref, cw_ref, wa_ref, lng_ref, lnb_ref, ws_ref, bst_ref, wb_ref,
                  bg_ref, wo_ref, l1g_ref, l1b_ref, wr_ref, br_ref,
                  x1_ref, ti_ref, tg_ref,
                  zc_ref, v_ref, gb_ref, *, tiles_per_seq, alpha, n_exp):
    i = pl.program_id(0)
    tm, d = x_ref.shape
    cw = cw_ref.shape[1]
    sw = lng_ref.shape[1]
    n_groups = sw // SGU_GROUP_CH
    n_blk = tm // SGU_BLOCK

    z = p_ref[:, 0:cw].astype(_F32) * p_ref[:, cw:2 * cw].astype(_F32)

    @pl.when(i % tiles_per_seq == 0)
    def _():
        zc_ref[...] = jnp.zeros_like(zc_ref)

    prev1 = zc_ref[7:8, :]
    prev2 = zc_ref[6:7, :]
    row = lax.broadcasted_iota(jnp.int32, (tm, cw), 0)
    z1 = jnp.where(row == 0, prev1, pltpu.roll(z, 1, 0))
    z2 = jnp.where(row == 0, prev2, jnp.where(row == 1, prev1, pltpu.roll(z, 2, 0)))
    zc_ref[...] = z[tm - 8:tm, :]
    conv = cw_ref[0:1, :] * z2 + cw_ref[1:2, :] * z1 + cw_ref[2:3, :] * z
    post = p_ref[:, 2 * cw:3 * cw].astype(_F32)
    ya = jnp.dot((post * conv).astype(_BF16), wa_ref[...], preferred_element_type=_F32)

    c2 = 3 * cw
    v = _gelu(p_ref[:, c2 + sw:c2 + 2 * sw].astype(_F32))
    v_ref[...] = _layer_norm(v, lng_ref[...], lnb_ref[...]).astype(_BF16)
    pos_p = lax.broadcasted_iota(jnp.int32, (SGU_BLOCK, SGU_BLOCK), 0)
    pos_q = lax.broadcasted_iota(jnp.int32, (SGU_BLOCK, SGU_BLOCK), 1)
    causal = (pos_q // CHUNK) <= (pos_p // CHUNK)
    for g in range(n_groups):
        cs = slice(g * SGU_GROUP_CH, (g + 1) * SGU_GROUP_CH)
        wm = jnp.where(causal, ws_ref[g], 0.0).astype(_BF16)
        vcat = jnp.concatenate(
            [v_ref[n * SGU_BLOCK:(n + 1) * SGU_BLOCK, cs] for n in range(n_blk)], axis=1)
        mixed = jnp.dot(wm, vcat, preferred_element_type=_F32) + bst_ref[:, g:g + 1]
        for n in range(n_blk):
            rs = slice(n * SGU_BLOCK, (n + 1) * SGU_BLOCK)
            u = _gelu(p_ref[rs, c2 + g * SGU_GROUP_CH:c2 + (g + 1) * SGU_GROUP_CH].astype(_F32))
            gb_ref[rs, cs] = (u * mixed[:, n * SGU_BLOCK:(n + 1) * SGU_BLOCK]).astype(_BF16)
    yb = jnp.dot(gb_ref[...], wb_ref[...], preferred_element_type=_F32)

    c3 = c2 + 2 * sw
    ga = jax.nn.sigmoid(p_ref[:, c3:c3 + d].astype(_F32) + bg_ref[:, 0:d])
    gbr = jax.nn.sigmoid(p_ref[:, c3 + d:c3 + 2 * d].astype(_F32) + bg_ref[:, d:2 * d])
    m = (ga * ya + gbr * yb).astype(_BF16)
    mix = jnp.dot(m, wo_ref[...], preferred_element_type=_F32)
    x1 = _layer_norm(alpha * x_ref[...] + mix, l1g_ref[...], l1b_ref[...])
    x1_ref[...] = x1

    logits = jnp.dot(x1.astype(_BF16), wr_ref[...], preferred_element_type=_F32) + br_ref[...]
    lane = lax.broadcasted_iota(jnp.int32, logits.shape, 1).astype(_F32)
    neg_inf = jnp.float32(-jnp.inf)
    vals = jnp.where(lane < n_exp, logits, neg_inf)
    top_i = jnp.zeros(logits.shape, _F32)
    top_e = jnp.zeros(logits.shape, _F32)
    v_max = None
    for k in range(TOP_K):
        vk = jnp.max(vals, axis=1, keepdims=True)
        ik = jnp.min(jnp.where(vals == vk, lane, float(LANES)), axis=1, keepdims=True)
        vals = jnp.where(lane == ik, neg_inf, vals)
        if k == 0:
            v_max = vk
        top_i = jnp.where(lane == k, ik, top_i)
        top_e = jnp.where(lane == k, jnp.exp(vk - v_max), top_e)
    ti_ref[...] = top_i.astype(jnp.int32)
    tg_ref[...] = top_e / jnp.sum(top_e, axis=1, keepdims=True)


def _mixer(p, x2d, conv_w, w_a, ln_g, ln_b, w_s, b_s_t, w_b, b_gate, w_o, l1g, l1b,
           w_r, b_r, *, seq, alpha, n_exp):
    t, d = x2d.shape
    nc = p.shape[1]
    cw = conv_w.shape[1]
    sw = ln_g.shape[1]
    tm = _tile(seq, MIXER_ROWS)
    assert tm % SGU_BLOCK == 0 and tm >= 8

    def const(shape):
        nd = len(shape)
        return pl.BlockSpec(shape, lambda i: (0,) * nd, pipeline_mode=pl.Buffered(1))

    weights_bytes = (cw * d + sw * d + d * d + d * LANES) * 2
    est = (2 * tm * nc * 2 + 4 * tm * d * 4 + weights_bytes + 2 * tm * sw * 2
           + 4 * tm * LANES * 4)
    return pl.pallas_call(
        functools.partial(_mixer_kernel, tiles_per_seq=seq // tm, alpha=alpha, n_exp=n_exp),
        grid=(t // tm,),
        in_specs=[
            pl.BlockSpec((tm, nc), lambda i: (i, 0)),
            pl.BlockSpec((tm, d), lambda i: (i, 0)),
            const(conv_w.shape), const(w_a.shape), const(ln_g.shape), const(ln_b.shape),
            const(w_s.shape), const(b_s_t.shape), const(w_b.shape), const(b_gate.shape),
            const(w_o.shape), const(l1g.shape), const(l1b.shape), const(w_r.shape),
            const(b_r.shape),
        ],
        out_specs=[
            pl.BlockSpec((tm, d), lambda i: (i, 0)),
            pl.BlockSpec((tm, LANES), lambda i: (i, 0)),
            pl.BlockSpec((tm, LANES), lambda i: (i, 0)),
        ],
        out_shape=[
            jax.ShapeDtypeStruct((t, d), _F32),
            jax.ShapeDtypeStruct((t, LANES), jnp.int32),
            jax.ShapeDtypeStruct((t, LANES), _F32),
        ],
        scratch_shapes=[
            pltpu.VMEM((8, cw), _F32),
            pltpu.VMEM((tm, sw), _BF16),
            pltpu.VMEM((tm, sw), _BF16),
        ],
        compiler_params=pltpu.CompilerParams(
            dimension_semantics=("arbitrary",),
            vmem_limit_bytes=_vmem_limit(est + 24 * 1024 * 1024),
        ),
        name="mixer",
    )(p, x2d, conv_w, w_a, ln_g, ln_b, w_s, b_s_t, w_b, b_gate, w_o, l1g, l1b, w_r, b_r)


def _moe_kernel(be_ref, bv_ref,
                tokc_ref, tokn_ref, dstp_ref, dstc_ref, x_hbm, wg_ref, wl_ref, bug_ref,
                bul_ref, wd_ref, bd_ref,
                y_hbm,
                xbuf, xb16, acc_ref, obuf, gsem, ssem, *, n_j):
    del be_ref
    i = pl.program_id(0)
    j = pl.program_id(1)
    n_blocks = pl.num_programs(0)
    slot = i % 2
    other = 1 - slot
    nv = bv_ref[i]
    n_groups, _, d = xbuf.shape[1:]
    tmb = n_groups * SUBLANES
    groups_per_step = n_groups // n_j

    def gather_copy(tok, g, u, s):
        return pltpu.make_async_copy(
            x_hbm.at[lax.shift_right_logical(tok, 3), pl.ds(lax.bitwise_and(tok, 7), 1)],
            xbuf.at[s, g, pl.ds(u, 1)], gsem.at[s])

    def scatter_copy(dst, g, u, s):
        return pltpu.make_async_copy(
            obuf.at[s, g, pl.ds(u, 1)],
            y_hbm.at[lax.shift_right_logical(dst, 3), pl.ds(lax.bitwise_and(dst, 7), 1)],
            ssem.at[s])

    def wait_gather(s):
        pltpu.make_async_copy(x_hbm.at[pl.ds(0, n_groups)], xbuf.at[s], gsem.at[s]).wait()

    def wait_scatter(s):
        pltpu.make_async_copy(obuf.at[s], y_hbm.at[pl.ds(0, n_groups)], ssem.at[s]).wait()

    def issue_rows(g0, n_g, tok_ref, dst_ref, s_gather, s_scatter):
        for gg in range(n_g):
            for u in range(SUBLANES):
                r = (g0 + gg) * SUBLANES + u
                if tok_ref is not None:
                    gather_copy(tok_ref[0, r], g0 + gg, u, s_gather).start()
                if dst_ref is not None:
                    scatter_copy(dst_ref[0, r], g0 + gg, u, s_scatter).start()

    def issue_block(tok_ref, dst_ref, s):
        def body(g, c):
            issue_rows(g, 1, tok_ref, dst_ref, s, s)
            return c
        lax.fori_loop(0, n_groups, body, 0)

    @pl.when(jnp.logical_and(j == 0, nv > 0))
    def _():
        @pl.when(i == 0)
        def _():
            obuf[...] = jnp.zeros_like(obuf)
            issue_block(tokc_ref, None, 0)

        @pl.when(i > 0)
        def _():
            wait_scatter(slot)

        wait_gather(slot)
        xb16[...] = xbuf[slot].reshape(tmb, d).astype(_BF16)
        acc_ref[...] = jnp.zeros_like(acc_ref)

    @pl.when(nv > 0)
    def _():
        g0 = pl.multiple_of(j * groups_per_step, groups_per_step)
        issue_rows(g0, groups_per_step, tokn_ref, dstp_ref, other, other)

        xb = xb16[...]
        a_glu = jnp.dot(xb, wg_ref[...], preferred_element_type=_F32) + bug_ref[...]
        a_lin = jnp.dot(xb, wl_ref[...], preferred_element_type=_F32) + bul_ref[...]
        glu = jnp.minimum(a_glu, SWIGLU_LIMIT)
        lin = jnp.clip(a_lin, -SWIGLU_LIMIT, SWIGLU_LIMIT)
        h = glu * jax.nn.sigmoid(SWIGLU_ALPHA * glu) * (lin + 1.0)
        acc_ref[...] += jnp.dot(h.astype(_BF16), wd_ref[...], preferred_element_type=_F32)

        @pl.when(j == n_j - 1)
        def _():
            obuf[slot] = (acc_ref[...] + bd_ref[...]).reshape(n_groups, SUBLANES, d)

            is_last = jnp.logical_or(
                i == n_blocks - 1, bv_ref[jnp.minimum(i + 1, n_blocks - 1)] == 0)

            @pl.when(is_last)
            def _():
                wait_gather(other)
                wait_scatter(other)
                issue_block(None, dstc_ref, slot)
                wait_scatter(slot)


def _moe(x1, w_up, b_up, w_down, b_down, blk_expert, blk_valid, row_tok, row_dst):
    t, d = x1.shape
    n_exp, _, f2 = w_up.shape
    f = f2 // 2
    n_blocks = blk_expert.shape[0]
    tmb = row_dst.shape[-1]
    fc = _tile(f, MOE_FF_COLS)
    n_j = f // fc
    assert n_j >= 2

    def jj(i, j, bv):
        return jnp.where(bv[i] > 0, j, n_j - 1)

    assert tmb % (SUBLANES * n_j) == 0 and t % SUBLANES == 0
    n_groups = tmb // SUBLANES
    smem = functools.partial(pl.BlockSpec, memory_space=pltpu.SMEM)
    est = (2 * tmb * d * 4 + tmb * d * 2 + tmb * d * 4 + 2 * tmb * d * 4
           + 2 * (2 * d * fc + fc * d) * 2 + 4 * tmb * fc * 4 + 2 * tmb * d * 4)
    y_rows = pl.pallas_call(
        functools.partial(_moe_kernel, n_j=n_j),
        grid_spec=pltpu.PrefetchScalarGridSpec(
            num_scalar_prefetch=2,
            grid=(n_blocks, n_j),
            in_specs=[
                smem((None, 1, tmb), lambda i, j, be, bv: (i, 0, 0)),
                smem((None, 1, tmb), lambda i, j, be, bv: (i + 1, 0, 0)),
                smem((None, 1, tmb), lambda i, j, be, bv: (i, 0, 0)),
                smem((None, 1, tmb), lambda i, j, be, bv: (i + 1, 0, 0)),
                pl.BlockSpec(memory_space=pl.ANY),
                pl.BlockSpec((None, d, fc), lambda i, j, be, bv: (be[i], 0, jj(i, j, bv))),
                pl.BlockSpec((None, d, fc), lambda i, j, be, bv: (be[i], 0, n_j + jj(i, j, bv))),
                pl.BlockSpec((None, 1, fc), lambda i, j, be, bv: (be[i], 0, jj(i, j, bv))),
                pl.BlockSpec((None, 1, fc), lambda i, j, be, bv: (be[i], 0, n_j + jj(i, j, bv))),
                pl.BlockSpec((None, fc, d), lambda i, j, be, bv: (be[i], jj(i, j, bv), 0)),
                pl.BlockSpec((None, 1, d), lambda i, j, be, bv: (be[i], 0, 0)),
            ],
            out_specs=pl.BlockSpec(memory_space=pl.ANY),
            scratch_shapes=[
                pltpu.VMEM((2, n_groups, SUBLANES, d), _F32),
                pltpu.VMEM((tmb, d), _BF16),
                pltpu.VMEM((tmb, d), _F32),
                pltpu.VMEM((2, n_groups, SUBLANES, d), _F32),
                pltpu.SemaphoreType.DMA((2,)),
                pltpu.SemaphoreType.DMA((2,)),
            ],
        ),
        out_shape=jax.ShapeDtypeStruct(((TOP_K * t + tmb) // SUBLANES, SUBLANES, d), _F32),
        compiler_params=pltpu.CompilerParams(
            dimension_semantics=("arbitrary", "arbitrary"),
            vmem_limit_bytes=_vmem_limit(est + 8 * 1024 * 1024),
        ),
        name="moe",
    )(blk_expert, blk_valid, row_tok, row_tok, row_dst, row_dst,
      x1.reshape(t // SUBLANES, SUBLANES, d), w_up, w_up,
      b_up.reshape(n_exp, 1, f2), b_up.reshape(n_exp, 1, f2), w_down,
      b_down.reshape(n_exp, 1, d))
    return y_rows.reshape(TOP_K * t + tmb, d)


def _routing_tables(top_idx, n_exp, tmb):
    t = top_idx.shape[0]
    n_assign = t * TOP_K
    e_flat = top_idx.reshape(-1)
    order = jnp.argsort(e_flat).astype(jnp.int32)
    counts = jnp.bincount(e_flat, length=n_exp).astype(jnp.int32)
    padded = ((counts + tmb - 1) // tmb) * tmb
    starts = jnp.cumsum(counts) - counts
    pends = jnp.cumsum(padded)
    pstarts = pends - padded
    n_blocks = -(-n_assign // tmb) + n_exp
    blk_start = jnp.arange(n_blocks, dtype=jnp.int32) * tmb
    blk_expert = jnp.minimum(
        jnp.searchsorted(pends, blk_start, side="right"), n_exp - 1).astype(jnp.int32)
    blk_valid = jnp.clip(counts[blk_expert] - (blk_start - pstarts[blk_expert]), 0, tmb)
    blk_valid = blk_valid.astype(jnp.int32)
    rows = jnp.arange(n_blocks * tmb, dtype=jnp.int32)
    row_e = jnp.repeat(blk_expert, tmb)
    rank = rows - pstarts[row_e]
    valid = rank < counts[row_e]
    assign = order[jnp.clip(starts[row_e] + rank, 0, n_assign - 1)]
    tok = assign // TOP_K
    slot_k = assign % TOP_K
    spare = TOP_K * t + rows % tmb
    row_tok = jnp.where(valid, tok, 0).astype(jnp.int32)
    row_dst = jnp.where(valid, slot_k * t + tok, spare).astype(jnp.int32)
    row_tok = jnp.concatenate([row_tok, jnp.zeros((tmb,), jnp.int32)])
    row_dst = jnp.concatenate([spare[:tmb], row_dst])
    return (blk_expert, blk_valid, row_tok.reshape(n_blocks + 1, 1, tmb),
            row_dst.reshape(n_blocks + 1, 1, tmb))


def _combine_kernel(*refs, alpha):
    y_refs = refs[:TOP_K]
    x1_ref, g_ref, l2g_ref, l2b_ref, o_ref = refs[TOP_K:]
    y = g_ref[:, 0:1] * y_refs[0][...]
    for k in range(1, TOP_K):
        y = y + g_ref[:, k:k + 1] * y_refs[k][...]
    o_ref[...] = _layer_norm(alpha * x1_ref[...] + y, l2g_ref[...], l2b_ref[...])


def _combine(y_rows, x1, gates, l2g, l2b, *, alpha):
    t, d = x1.shape
    tm = _tile(t, COMBINE_ROWS)
    tiles = t // tm
    est = 2 * (TOP_K + 2) * tm * d * 4 + 2 * tm * LANES * 4 + 4 * tm * d * 4
    return pl.pallas_call(
        functools.partial(_combine_kernel, alpha=alpha),
        grid=(tiles,),
        in_specs=[
            *[pl.BlockSpec((tm, d), functools.partial(lambda i, k: (k * tiles + i, 0), k=k))
              for k in range(TOP_K)],
            pl.BlockSpec((tm, d), lambda i: (i, 0)),
            pl.BlockSpec((tm, LANES), lambda i: (i, 0)),
            pl.BlockSpec((1, d), lambda i: (0, 0)),
            pl.BlockSpec((1, d), lambda i: (0, 0)),
        ],
        out_specs=pl.BlockSpec((tm, d), lambda i: (i, 0)),
        out_shape=jax.ShapeDtypeStruct((t, d), _F32),
        compiler_params=pltpu.CompilerParams(
            dimension_semantics=("arbitrary",),
            vmem_limit_bytes=_vmem_limit(est + 8 * 1024 * 1024),
        ),
        name="combine",
    )(*([y_rows] * TOP_K), x1, gates, l2g, l2b)


def kernel(x, w_in, conv_w, w_a_out, ln_v_g, ln_v_b, w_s, b_s, w_b_out, b_gate, w_o, ln1_g,
           ln1_b, w_router, b_router, w_up, b_up, w_down, b_down, ln2_g, ln2_b):
    bsz, seq, d = x.shape
    depth = w_in.shape[0]
    n_exp = w_router.shape[-1]
    alpha = (2.0 * depth) ** 0.25
    t = bsz * seq
    assert n_exp <= LANES and t * TOP_K % MOE_ROWS == 0
    h = x.reshape(t, d)
    for l in range(depth):
        p = _inproj(h, w_in[l].astype(_BF16))
        w_r = jnp.pad(w_router[l], ((0, 0), (0, LANES - n_exp))).astype(_BF16)
        b_r = jnp.pad(b_router[l], (0, LANES - n_exp)).reshape(1, LANES)
        x1, top_i, top_g = _mixer(
            p, h, conv_w[l], w_a_out[l].astype(_BF16), ln_v_g[l].reshape(1, -1),
            ln_v_b[l].reshape(1, -1), w_s[l], b_s[l].T, w_b_out[l].astype(_BF16),
            b_gate[l].reshape(1, -1), w_o[l].astype(_BF16), ln1_g[l].reshape(1, -1),
            ln1_b[l].reshape(1, -1), w_r, b_r, seq=seq, alpha=alpha, n_exp=n_exp)
        tables = _routing_tables(top_i[:, :TOP_K], n_exp, MOE_ROWS)
        y_rows = _moe(x1, w_up[l].astype(_BF16), b_up[l], w_down[l].astype(_BF16), b_down[l],
                      *tables)
        h = _combine(y_rows, x1, top_g, ln2_g[l].reshape(1, -1),
                     ln2_b[l].reshape(1, -1), alpha=alpha)
    return h.reshape(bsz, seq, d)
```

```python
import functools

import jax
import jax.numpy as jnp
from jax import lax
from jax.experimental import pallas as pl
from jax.experimental.pallas import tpu as pltpu

CHUNK = 64
CONV_K = 3
SGU_BLOCK = 128
SGU_GROUP_CH = 128
TOP_K = 4
SWIGLU_ALPHA = 1.702
SWIGLU_LIMIT = 7.0
LN_EPS = 1e-5

LANES = 128
SUBLANES = 8
BF16_SUBLANES = 16
VMEM_BYTES_V7X = 64 * 1024 * 1024

INPROJ_ROWS = 1024
INPROJ_COLS = 1024
MIXER_ROWS = 256
MOE_ROWS = 512
MOE_FF_COLS = 512
COMBINE_ROWS = 256

_F32 = jnp.float32
_BF16 = jnp.bfloat16


def _tile(n, pref):
    if n <= pref:
        return n
    for c in range(pref - pref % LANES, 0, -LANES):
        if n % c == 0:
            return c
    raise ValueError(f"no lane-aligned tile of {n} within {pref}")


def _vmem_limit(est_bytes):
    return int(min(VMEM_BYTES_V7X - 4 * 1024 * 1024, est_bytes))


def _layer_norm(h, g, b):
    mu = jnp.mean(h, axis=-1, keepdims=True)
    c = h - mu
    var = jnp.mean(c * c, axis=-1, keepdims=True)
    return c * lax.rsqrt(var + LN_EPS) * g + b


def _gelu(x):
    return 0.5 * x * (1.0 + lax.erf(x * (2.0 ** -0.5)))


def _side_cast_plan(rows, n_steps):
    n = min(n_steps, rows // BF16_SUBLANES)
    while rows % n or (rows // n) % BF16_SUBLANES:
        n -= 1
    return n, rows // n


def _side_cast_specs(side, n_chunks, rows_per_chunk, step_of):
    cols = side.shape[1]

    def index_map(*grid_idx):
        return (jnp.minimum(step_of(*grid_idx), n_chunks - 1), 0)
    spec = pl.BlockSpec((rows_per_chunk, cols), index_map)
    return spec, spec, jax.ShapeDtypeStruct(side.shape, _BF16)


def _inproj_kernel(x_ref, w_ref, side_ref, o_ref, side_o_ref, xb_ref, *, n_side):
    j = pl.program_id(1)

    @pl.when(j == 0)
    def _():
        xb_ref[...] = x_ref[...].astype(_BF16)

    o_ref[...] = jnp.dot(xb_ref[...], w_ref[...], preferred_element_type=_F32).astype(o_ref.dtype)

    @pl.when(pl.program_id(0) * pl.num_programs(1) + j < n_side)
    def _():
        side_o_ref[...] = side_ref[...].astype(_BF16)


def _inproj(x2d, w_in, side):
    t, d = x2d.shape
    nc = w_in.shape[1]
    tm = _tile(t, INPROJ_ROWS)
    tn = _tile(nc, INPROJ_COLS)
    n_j = nc // tn
    n_side, side_rows = _side_cast_plan(side.shape[0], (t // tm) * n_j)
    side_in, side_out, side_shape = _side_cast_specs(
        side, n_side, side_rows, lambda i, j: i * n_j + j)
    est = (2 * tm * d * 4 + tm * d * 2 + 2 * d * tn * 2 + 2 * tm * tn * 2 + 2 * tm * tn * 4
           + 2 * side_rows * side.shape[1] * 6)
    return pl.pallas_call(
        functools.partial(_inproj_kernel, n_side=n_side),
        grid=(t // tm, n_j),
        in_specs=[
            pl.BlockSpec((tm, d), lambda i, j: (i, 0)),
            pl.BlockSpec((d, tn), lambda i, j: (0, j)),
            side_in,
        ],
        out_specs=[pl.BlockSpec((tm, tn), lambda i, j: (i, j)), side_out],
        out_shape=[jax.ShapeDtypeStruct((t, nc), _BF16), side_shape],
        scratch_shapes=[pltpu.VMEM((tm, d), _BF16)],
        compiler_params=pltpu.CompilerParams(
            dimension_semantics=("arbitrary", "arbitrary"),
            vmem_limit_bytes=_vmem_limit(est + 8 * 1024 * 1024),
        ),
        name="inproj",
    )(x2d, w_in, side)


def _mixer_kernel(p_ref, x_ref, cw_ref, wa_ref, lng_ref, lnb_ref, ws_ref, bst_ref, wb_ref,
                  bg_ref, wo_ref, l1g_ref, l1b_ref, wr_ref, br_ref, side_ref,
                  x1_ref, ti_ref, tg_ref, side_o_ref,
                  zc_ref, v_ref, gb_ref, *, tiles_per_seq, alpha, n_exp, n_side):
    i = pl.program_id(0)

    @pl.when(i < n_side)
    def _():
        side_o_ref[...] = side_ref[...].astype(_BF16)

    tm, d = x_ref.shape
    cw = cw_ref.shape[1]
    sw = lng_ref.shape[1]
    n_groups = sw // SGU_GROUP_CH
    n_blk = tm // SGU_BLOCK

    z = p_ref[:, 0:cw].astype(_F32) * p_ref[:, cw:2 * cw].astype(_F32)

    @pl.when(i % tiles_per_seq == 0)
    def _():
        zc_ref[...] = jnp.zeros_like(zc_ref)

    prev1 = zc_ref[7:8, :]
    prev2 = zc_ref[6:7, :]
    row = lax.broadcasted_iota(jnp.int32, (tm, cw), 0)
    z1 = jnp.where(row == 0, prev1, pltpu.roll(z, 1, 0))
    z2 = jnp.where(row == 0, prev2, jnp.where(row == 1, prev1, pltpu.roll(z, 2, 0)))
    zc_ref[...] = z[tm - 8:tm, :]
    conv = cw_ref[0:1, :] * z2 + cw_ref[1:2, :] * z1 + cw_ref[2:3, :] * z
    post = p_ref[:, 2 * cw:3 * cw].astype(_F32)
    ya = jnp.dot((post * conv).astype(_BF16), wa_ref[...], preferred_element_type=_F32)

    c2 = 3 * cw
    v = _gelu(p_ref[:, c2 + sw:c2 + 2 * sw].astype(_F32))
    v_ref[...] = _layer_norm(v, lng_ref[...], lnb_ref[...]).astype(_BF16)
    pos_p = lax.broadcasted_iota(jnp.int32, (SGU_BLOCK, SGU_BLOCK), 0)
    pos_q = lax.broadcasted_iota(jnp.int32, (SGU_BLOCK, SGU_BLOCK), 1)
    causal = (pos_q // CHUNK) <= (pos_p // CHUNK)
    for g in range(n_groups):
        cs = slice(g * SGU_GROUP_CH, (g + 1) * SGU_GROUP_CH)
        wm = jnp.where(causal, ws_ref[g], 0.0).astype(_BF16)
        vcat = jnp.concatenate(
            [v_ref[n * SGU_BLOCK:(n + 1) * SGU_BLOCK, cs] for n in range(n_blk)], axis=1)
        mixed = jnp.dot(wm, vcat, preferred_element_type=_F32) + bst_ref[:, g:g + 1]
        for n in range(n_blk):
            rs = slice(n * SGU_BLOCK, (n + 1) * SGU_BLOCK)
            u = _gelu(p_ref[rs, c2 + g * SGU_GROUP_CH:c2 + (g + 1) * SGU_GROUP_CH].astype(_F32))
            gb_ref[rs, cs] = (u * mixed[:, n * SGU_BLOCK:(n + 1) * SGU_BLOCK]).astype(_BF16)
    yb = jnp.dot(gb_ref[...], wb_ref[...], preferred_element_type=_F32)

    c3 = c2 + 2 * sw
    ga = jax.nn.sigmoid(p_ref[:, c3:c3 + d].astype(_F32) + bg_ref[:, 0:d])
    gbr = jax.nn.sigmoid(p_ref[:, c3 + d:c3 + 2 * d].astype(_F32) + bg_ref[:, d:2 * d])
    m = (ga * ya + gbr * yb).astype(_BF16)
    mix = jnp.dot(m, wo_ref[...], preferred_element_type=_F32)
    x1 = _layer_norm(alpha * x_ref[...] + mix, l1g_ref[...], l1b_ref[...])
    x1_ref[...] = x1

    logits = jnp.dot(x1.astype(_BF16), wr_ref[...], preferred_element_type=_F32) + br_ref[...]
    lane = lax.broadcasted_iota(jnp.int32, logits.shape, 1).astype(_F32)
    neg_inf = jnp.float32(-jnp.inf)
    vals = jnp.where(lane < n_exp, logits, neg_inf)
    top_i = jnp.zeros(logits.shape, _F32)
    top_e = jnp.zeros(logits.shape, _F32)
    v_max = None
    for k in range(TOP_K):
        vk = jnp.max(vals, axis=1, keepdims=True)
        ik = jnp.min(jnp.where(vals == vk, lane, float(LANES)), axis=1, keepdims=True)
        vals = jnp.where(lane == ik, neg_inf, vals)
        if k == 0:
            v_max = vk
        top_i = jnp.where(lane == k, ik, top_i)
        top_e = jnp.where(lane == k, jnp.exp(vk - v_max), top_e)
    ti_ref[...] = top_i.astype(jnp.int32)
    tg_ref[...] = top_e / jnp.sum(top_e, axis=1, keepdims=True)


def _mixer(p, x2d, conv_w, w_a, ln_g, ln_b, w_s, b_s_t, w_b, b_gate, w_o, l1g, l1b,
           w_r, b_r, side, *, seq, alpha, n_exp):
    t, d = x2d.shape
    nc = p.shape[1]
    cw = conv_w.shape[1]
    sw = ln_g.shape[1]
    tm = _tile(seq, MIXER_ROWS)
    assert tm % SGU_BLOCK == 0 and tm >= 8

    def const(shape):
        nd = len(shape)
        return pl.BlockSpec(shape, lambda i: (0,) * nd, pipeline_mode=pl.Buffered(1))

    n_side, side_rows = _side_cast_plan(side.shape[0], t // tm)
    side_in, side_out, side_shape = _side_cast_specs(side, n_side, side_rows, lambda i: i)
    weights_bytes = (cw * d + sw * d + d * d + d * LANES) * 2
    est = (2 * tm * nc * 2 + 4 * tm * d * 4 + weights_bytes + 2 * tm * sw * 2
           + 4 * tm * LANES * 4 + 2 * side_rows * side.shape[1] * 6)
    return pl.pallas_call(
        functools.partial(_mixer_kernel, tiles_per_seq=seq // tm, alpha=alpha, n_exp=n_exp,
                          n_side=n_side),
        grid=(t // tm,),
        in_specs=[
            pl.BlockSpec((tm, nc), lambda i: (i, 0)),
            pl.BlockSpec((tm, d), lambda i: (i, 0)),
            const(conv_w.shape), const(w_a.shape), const(ln_g.shape), const(ln_b.shape),
            const(w_s.shape), const(b_s_t.shape), const(w_b.shape), const(b_gate.shape),
            const(w_o.shape), const(l1g.shape), const(l1b.shape), const(w_r.shape),
            const(b_r.shape), side_in,
        ],
        out_specs=[
            pl.BlockSpec((tm, d), lambda i: (i, 0)),
            pl.BlockSpec((tm, LANES), lambda i: (i, 0)),
            pl.BlockSpec((tm, LANES), lambda i: (i, 0)),
            side_out,
        ],
        out_shape=[
            jax.ShapeDtypeStruct((t, d), _F32),
            jax.ShapeDtypeStruct((t, LANES), jnp.int32),
            jax.ShapeDtypeStruct((t, LANES), _F32),
            side_shape,
        ],
        scratch_shapes=[
            pltpu.VMEM((8, cw), _F32),
            pltpu.VMEM((tm, sw), _BF16),
            pltpu.VMEM((tm, sw), _BF16),
        ],
        compiler_params=pltpu.CompilerParams(
            dimension_semantics=("arbitrary",),
            vmem_limit_bytes=_vmem_limit(est + 24 * 1024 * 1024),
        ),
        name="mixer",
    )(p, x2d, conv_w, w_a, ln_g, ln_b, w_s, b_s_t, w_b, b_gate, w_o, l1g, l1b, w_r, b_r, side)


def _moe_kernel(be_ref, bv_ref,
                tokc_ref, tokn_ref, dstp_ref, dstc_ref, x_rows, x_hbm, wg_ref, wl_ref, bug_ref,
                bul_ref, wd_ref, bd_ref,
                y_hbm,
                xbuf, xb16, acc_ref, obuf, gsem, ssem, *, n_j):
    del be_ref
    i = pl.program_id(0)
    j = pl.program_id(1)
    n_blocks = pl.num_programs(0)
    slot = i % 2
    other = 1 - slot
    nv = bv_ref[i]
    n_groups, _, d = xbuf.shape[1:]
    tmb = n_groups * SUBLANES
    groups_per_step = n_groups // n_j

    def gather_copy(tok, g, u, s):
        return pltpu.make_async_copy(
            x_rows.at[pl.ds(tok, 1)], xbuf.at[s, g, pl.ds(u, 1)], gsem.at[s])

    def scatter_copy(dst, g, u, s):
        return pltpu.make_async_copy(
            obuf.at[s, g, pl.ds(u, 1)],
            y_hbm.at[lax.shift_right_logical(dst, 3), pl.ds(lax.bitwise_and(dst, 7), 1)],
            ssem.at[s])

    def wait_gather(s):
        pltpu.make_async_copy(x_hbm.at[pl.ds(0, n_groups)], xbuf.at[s], gsem.at[s]).wait()

    def wait_scatter(s):
        pltpu.make_async_copy(obuf.at[s], y_hbm.at[pl.ds(0, n_groups)], ssem.at[s]).wait()

    def issue_rows(g0, n_g, tok_ref, dst_ref, s_gather, s_scatter):
        for gg in range(n_g):
            for u in range(SUBLANES):
                r = (g0 + gg) * SUBLANES + u
                if tok_ref is not None:
                    gather_copy(tok_ref[0, r], g0 + gg, u, s_gather).start()
                if dst_ref is not None:
                    scatter_copy(dst_ref[0, r], g0 + gg, u, s_scatter).start()

    def issue_block(tok_ref, dst_ref, s):
        def body(g, c):
            issue_rows(g, 1, tok_ref, dst_ref, s, s)
            return c
        lax.fori_loop(0, n_groups, body, 0)

    @pl.when(jnp.logical_and(j == 0, nv > 0))
    def _():
        @pl.when(i == 0)
        def _():
            obuf[...] = jnp.zeros_like(obuf)
            issue_block(tokc_ref, None, 0)

        @pl.when(i > 0)
        def _():
            wait_scatter(slot)

        wait_gather(slot)
        xb16[...] = xbuf[slot].reshape(tmb, d).astype(_BF16)
        acc_ref[...] = jnp.zeros_like(acc_ref)

    @pl.when(nv > 0)
    def _():
        g0 = pl.multiple_of(j * groups_per_step, groups_per_step)
        issue_rows(g0, groups_per_step, tokn_ref, dstp_ref, other, other)

        xb = xb16[...]
        a_glu = jnp.dot(xb, wg_ref[...], preferred_element_type=_F32) + bug_ref[...]
        a_lin = jnp.dot(xb, wl_ref[...], preferred_element_type=_F32) + bul_ref[...]
        glu = jnp.minimum(a_glu, SWIGLU_LIMIT)
        lin = jnp.clip(a_lin, -SWIGLU_LIMIT, SWIGLU_LIMIT)
        h = glu * jax.nn.sigmoid(SWIGLU_ALPHA * glu) * (lin + 1.0)
        acc_ref[...] += jnp.dot(h.astype(_BF16), wd_ref[...], preferred_element_type=_F32)

        @pl.when(j == n_j - 1)
        def _():
            obuf[slot] = (acc_ref[...] + bd_ref[...]).reshape(n_groups, SUBLANES, d)

            is_last = jnp.logical_or(
                i == n_blocks - 1, bv_ref[jnp.minimum(i + 1, n_blocks - 1)] == 0)

            @pl.when(is_last)
            def _():
                wait_gather(other)
                wait_scatter(other)
                issue_block(None, dstc_ref, slot)
                wait_scatter(slot)


def _moe(x1, w_up, b_up, w_down, b_down, blk_expert, blk_valid, row_tok, row_dst):
    t, d = x1.shape
    n_exp, _, f2 = w_up.shape
    f = f2 // 2
    n_blocks = blk_expert.shape[0]
    tmb = row_dst.shape[-1]
    fc = _tile(f, MOE_FF_COLS)
    n_j = f // fc
    assert n_j >= 2

    def jj(i, j, bv):
        return jnp.where(bv[i] > 0, j, n_j - 1)

    assert tmb % (SUBLANES * n_j) == 0 and t % SUBLANES == 0
    n_groups = tmb // SUBLANES
    smem = functools.partial(pl.BlockSpec, memory_space=pltpu.SMEM)
    est = (2 * tmb * d * 4 + tmb * d * 2 + tmb * d * 4 + 2 * tmb * d * 4
           + 2 * (2 * d * fc + fc * d) * 2 + 4 * tmb * fc * 4 + 2 * tmb * d * 4)
    y_rows = pl.pallas_call(
        functools.partial(_moe_kernel, n_j=n_j),
        grid_spec=pltpu.PrefetchScalarGridSpec(
            num_scalar_prefetch=2,
            grid=(n_blocks, n_j),
            in_specs=[
                smem((None, 1, tmb), lambda i, j, be, bv: (i, 0, 0)),
                smem((None, 1, tmb), lambda i, j, be, bv: (i + 1, 0, 0)),
                smem((None, 1, tmb), lambda i, j, be, bv: (i, 0, 0)),
                smem((None, 1, tmb), lambda i, j, be, bv: (i + 1, 0, 0)),
                pl.BlockSpec(memory_space=pl.ANY),
                pl.BlockSpec(memory_space=pl.ANY),
                pl.BlockSpec((None, d, fc), lambda i, j, be, bv: (be[i], 0, jj(i, j, bv))),
                pl.BlockSpec((None, d, fc), lambda i, j, be, bv: (be[i], 0, n_j + jj(i, j, bv))),
                pl.BlockSpec((None, 1, fc), lambda i, j, be, bv: (be[i], 0, jj(i, j, bv))),
                pl.BlockSpec((None, 1, fc), lambda i, j, be, bv: (be[i], 0, n_j + jj(i, j, bv))),
                pl.BlockSpec((None, fc, d), lambda i, j, be, bv: (be[i], jj(i, j, bv), 0)),
                pl.BlockSpec((None, 1, d), lambda i, j, be, bv: (be[i], 0, 0)),
            ],
            out_specs=pl.BlockSpec(memory_space=pl.ANY),
            scratch_shapes=[
                pltpu.VMEM((2, n_groups, SUBLANES, d), _F32),
                pltpu.VMEM((tmb, d), _BF16),
                pltpu.VMEM((tmb, d), _F32),
                pltpu.VMEM((2, n_groups, SUBLANES, d), _F32),
                pltpu.SemaphoreType.DMA((2,)),
                pltpu.SemaphoreType.DMA((2,)),
            ],
        ),
        out_shape=jax.ShapeDtypeStruct(((TOP_K * t + tmb) // SUBLANES, SUBLANES, d), _F32),
        compiler_params=pltpu.CompilerParams(
            dimension_semantics=("arbitrary", "arbitrary"),
            vmem_limit_bytes=_vmem_limit(est + 8 * 1024 * 1024),
        ),
        name="moe",
    )(blk_expert, blk_valid, row_tok, row_tok, row_dst, row_dst,
      x1, x1.reshape(t // SUBLANES, SUBLANES, d), w_up, w_up,
      b_up.reshape(n_exp, 1, f2), b_up.reshape(n_exp, 1, f2), w_down,
      b_down.reshape(n_exp, 1, d))
    return y_rows.reshape(TOP_K * t + tmb, d)


def _routing_tables(top_idx, n_exp, tmb):
    t = top_idx.shape[0]
    n_assign = t * TOP_K
    e_flat = top_idx.reshape(-1)
    order = jnp.argsort(e_flat).astype(jnp.int32)
    counts = jnp.bincount(e_flat, length=n_exp).astype(jnp.int32)
    padded = ((counts + tmb - 1) // tmb) * tmb
    starts = jnp.cumsum(counts) - counts
    pends = jnp.cumsum(padded)
    pstarts = pends - padded
    n_blocks = -(-n_assign // tmb) + n_exp
    blk_start = jnp.arange(n_blocks, dtype=jnp.int32) * tmb
    blk_expert = jnp.minimum(
        jnp.sum(pends[None, :] <= blk_start[:, None], axis=1), n_exp - 1).astype(jnp.int32)
    blk_valid = jnp.clip(counts[blk_expert] - (blk_start - pstarts[blk_expert]), 0, tmb)
    blk_valid = blk_valid.astype(jnp.int32)
    rows = jnp.arange(n_blocks * tmb, dtype=jnp.int32)
    row_e = jnp.repeat(blk_expert, tmb)
    rank = rows - pstarts[row_e]
    valid = rank < counts[row_e]
    assign = order[jnp.clip(starts[row_e] + rank, 0, n_assign - 1)]
    tok = assign // TOP_K
    slot_k = assign % TOP_K
    spare = TOP_K * t + rows % tmb
    row_tok = jnp.where(valid, tok, 0).astype(jnp.int32)
    row_dst = jnp.where(valid, slot_k * t + tok, spare).astype(jnp.int32)
    row_tok = jnp.concatenate([row_tok, jnp.zeros((tmb,), jnp.int32)])
    row_dst = jnp.concatenate([spare[:tmb], row_dst])
    return (blk_expert, blk_valid, row_tok.reshape(n_blocks + 1, 1, tmb),
            row_dst.reshape(n_blocks + 1, 1, tmb))


def _combine_kernel(*refs, alpha):
    y_refs = refs[:TOP_K]
    x1_ref, g_ref, l2g_ref, l2b_ref, o_ref = refs[TOP_K:]
    y = g_ref[:, 0:1] * y_refs[0][...]
    for k in range(1, TOP_K):
        y = y + g_ref[:, k:k + 1] * y_refs[k][...]
    o_ref[...] = _layer_norm(alpha * x1_ref[...] + y, l2g_ref[...], l2b_ref[...])


def _combine(y_rows, x1, gates, l2g, l2b, *, alpha):
    t, d = x1.shape
    tm = _tile(t, COMBINE_ROWS)
    tiles = t // tm
    est = 2 * (TOP_K + 2) * tm * d * 4 + 2 * tm * LANES * 4 + 4 * tm * d * 4
    return pl.pallas_call(
        functools.partial(_combine_kernel, alpha=alpha),
        grid=(tiles,),
        in_specs=[
            *[pl.BlockSpec((tm, d), functools.partial(lambda i, k: (k * tiles + i, 0), k=k))
              for k in range(TOP_K)],
            pl.BlockSpec((tm, d), lambda i: (i, 0)),
            pl.BlockSpec((tm, LANES), lambda i: (i, 0)),
            pl.BlockSpec((1, d), lambda i: (0, 0)),
            pl.BlockSpec((1, d), lambda i: (0, 0)),
        ],
        out_specs=pl.BlockSpec((tm, d), lambda i: (i, 0)),
        out_shape=jax.ShapeDtypeStruct((t, d), _F32),
        compiler_params=pltpu.CompilerParams(
            dimension_semantics=("arbitrary",),
            vmem_limit_bytes=_vmem_limit(est + 8 * 1024 * 1024),
        ),
        name="combine",
    )(*([y_rows] * TOP_K), x1, gates, l2g, l2b)


def kernel(x, w_in, conv_w, w_a_out, ln_v_g, ln_v_b, w_s, b_s, w_b_out, b_gate, w_o, ln1_g,
           ln1_b, w_router, b_router, w_up, b_up, w_down, b_down, ln2_g, ln2_b):
    bsz, seq, d = x.shape
    depth = w_in.shape[0]
    n_exp = w_router.shape[-1]
    alpha = (2.0 * depth) ** 0.25
    t = bsz * seq
    assert n_exp <= LANES and t * TOP_K % MOE_ROWS == 0
    h = x.reshape(t, d)
    for l in range(depth):
        p, w_up_bf = _inproj(h, w_in[l].astype(_BF16), w_up[l].reshape(-1, w_up.shape[-1]))
        w_r = jnp.pad(w_router[l], ((0, 0), (0, LANES - n_exp))).astype(_BF16)
        b_r = jnp.pad(b_router[l], (0, LANES - n_exp)).reshape(1, LANES)
        x1, top_i, top_g, w_down_bf = _mixer(
            p, h, conv_w[l], w_a_out[l].astype(_BF16), ln_v_g[l].reshape(1, -1),
            ln_v_b[l].reshape(1, -1), w_s[l], b_s[l].T, w_b_out[l].astype(_BF16),
            b_gate[l].reshape(1, -1), w_o[l].astype(_BF16), ln1_g[l].reshape(1, -1),
            ln1_b[l].reshape(1, -1), w_r, b_r, w_down[l].reshape(-1, w_down.shape[-1]),
            seq=seq, alpha=alpha, n_exp=n_exp)
        tables = _routing_tables(top_i[:, :TOP_K], n_exp, MOE_ROWS)
        y_rows = _moe(x1, w_up_bf.reshape(w_up.shape[1:]), b_up[l],
                      w_down_bf.reshape(w_down.shape[1:]), b_down[l], *tables)
        h = _combine(y_rows, x1, top_g, ln2_g[l].reshape(1, -1),
                     ln2_b[l].reshape(1, -1), alpha=alpha)
    return h.reshape(bsz, seq, d)
```

```python
import functools

import jax
import jax.numpy as jnp
from jax import lax
from jax.experimental import pallas as pl
from jax.experimental.pallas import tpu as pltpu

CHUNK = 64
CONV_K = 3
SGU_BLOCK = 128
SGU_GROUP_CH = 128
TOP_K = 4
SWIGLU_ALPHA = 1.702
SWIGLU_LIMIT = 7.0
LN_EPS = 1e-5

LANES = 128
SUBLANES = 8
BF16_SUBLANES = 16
VMEM_BYTES_V7X = 64 * 1024 * 1024

INPROJ_ROWS = 1024
INPROJ_COLS = 1024
MIXER_ROWS = 256
MOE_ROWS = 512
MOE_FF_COLS = 1024
ISSUE_STEPS = 1
COMBINE_ROWS = 256

_F32 = jnp.float32
_BF16 = jnp.bfloat16


def _tile(n, pref):
    if n <= pref:
        return n
    for c in range(pref - pref % LANES, 0, -LANES):
        if n % c == 0:
            return c
    raise ValueError(f"no lane-aligned tile of {n} within {pref}")


def _vmem_limit(est_bytes):
    return int(min(VMEM_BYTES_V7X - 4 * 1024 * 1024, est_bytes))


def _layer_norm(h, g, b):
    mu = jnp.mean(h, axis=-1, keepdims=True)
    c = h - mu
    var = jnp.mean(c * c, axis=-1, keepdims=True)
    return c * lax.rsqrt(var + LN_EPS) * g + b


def _gelu(x):
    return 0.5 * x * (1.0 + lax.erf(x * (2.0 ** -0.5)))


def _pack_bf16_halves(a_bf16):
    half = a_bf16.shape[1] // 2
    lo = lax.bitcast_convert_type(a_bf16[:, :half].astype(_F32), jnp.uint32)
    hi = lax.bitcast_convert_type(a_bf16[:, half:].astype(_F32), jnp.uint32)
    return lax.shift_right_logical(lo, jnp.uint32(16)) | (hi & jnp.uint32(0xFFFF0000))


def _unpack_bf16_halves(w_u32):
    lo = lax.bitcast_convert_type(lax.shift_left(w_u32, jnp.uint32(16)), _F32)
    hi = lax.bitcast_convert_type(w_u32 & jnp.uint32(0xFFFF0000), _F32)
    return lo, hi


def _side_cast_plan(rows, n_steps):
    n = min(n_steps, rows // BF16_SUBLANES)
    while rows % n or (rows // n) % BF16_SUBLANES:
        n -= 1
    return n, rows // n


def _side_cast_specs(side, n_chunks, rows_per_chunk, step_of):
    cols = side.shape[1]

    def index_map(*grid_idx):
        return (jnp.minimum(step_of(*grid_idx), n_chunks - 1), 0)
    spec = pl.BlockSpec((rows_per_chunk, cols), index_map)
    return spec, spec, jax.ShapeDtypeStruct(side.shape, _BF16)


def _inproj_kernel(x_ref, w_ref, side_ref, o_ref, side_o_ref, xb_ref, *, n_side):
    j = pl.program_id(1)

    @pl.when(j == 0)
    def _():
        xb_ref[...] = x_ref[...].astype(_BF16)

    o_ref[...] = jnp.dot(xb_ref[...], w_ref[...], preferred_element_type=_F32).astype(o_ref.dtype)

    @pl.when(pl.program_id(0) * pl.num_programs(1) + j < n_side)
    def _():
        side_o_ref[...] = side_ref[...].astype(_BF16)


def _inproj(x2d, w_in, side):
    t, d = x2d.shape
    nc = w_in.shape[1]
    tm = _tile(t, INPROJ_ROWS)
    tn = _tile(nc, INPROJ_COLS)
    n_j = nc // tn
    n_side, side_rows = _side_cast_plan(side.shape[0], (t // tm) * n_j)
    side_in, side_out, side_shape = _side_cast_specs(
        side, n_side, side_rows, lambda i, j: i * n_j + j)
    est = (2 * tm * d * 4 + tm * d * 2 + 2 * d * tn * 2 + 2 * tm * tn * 2 + 2 * tm * tn * 4
           + 2 * side_rows * side.shape[1] * 6)
    return pl.pallas_call(
        functools.partial(_inproj_kernel, n_side=n_side),
        grid=(t // tm, n_j),
        in_specs=[
            pl.BlockSpec((tm, d), lambda i, j: (i, 0)),
            pl.BlockSpec((d, tn), lambda i, j: (0, j)),
            side_in,
        ],
        out_specs=[pl.BlockSpec((tm, tn), lambda i, j: (i, j)), side_out],
        out_shape=[jax.ShapeDtypeStruct((t, nc), _BF16), side_shape],
        scratch_shapes=[pltpu.VMEM((tm, d), _BF16)],
        compiler_params=pltpu.CompilerParams(
            dimension_semantics=("arbitrary", "arbitrary"),
            vmem_limit_bytes=_vmem_limit(est + 8 * 1024 * 1024),
        ),
        name="inproj",
    )(x2d, w_in, side)


def _mixer_kernel(p_ref, x_ref, cw_ref, wa_ref, lng_ref, lnb_ref, ws_ref, bst_ref, wb_ref,
                  bg_ref, wo_ref, l1g_ref, l1b_ref, wr_ref, br_ref, side_ref,
                  x1_ref, x1p_ref, ti_ref, tg_ref, side_o_ref,
                  zc_ref, v_ref, gb_ref, *, tiles_per_seq, alpha, n_exp, n_side, n_steps):
    i = pl.program_id(0)

    if n_side == n_steps:
        side_o_ref[...] = side_ref[...].astype(_BF16)
    else:
        @pl.when(i < n_side)
        def _():
            side_o_ref[...] = side_ref[...].astype(_BF16)

    tm, d = x_ref.shape
    cw = cw_ref.shape[1]
    sw = lng_ref.shape[1]
    n_groups = sw // SGU_GROUP_CH
    n_blk = tm // SGU_BLOCK

    z = p_ref[:, 0:cw].astype(_F32) * p_ref[:, cw:2 * cw].astype(_F32)

    @pl.when(i % tiles_per_seq == 0)
    def _():
        zc_ref[...] = jnp.zeros_like(zc_ref)

    prev1 = zc_ref[7:8, :]
    prev2 = zc_ref[6:7, :]
    row = lax.broadcasted_iota(jnp.int32, (tm, cw), 0)
    z1 = jnp.where(row == 0, prev1, pltpu.roll(z, 1, 0))
    z2 = jnp.where(row == 0, prev2, jnp.where(row == 1, prev1, pltpu.roll(z, 2, 0)))
    zc_ref[...] = z[tm - 8:tm, :]
    conv = cw_ref[0:1, :] * z2 + cw_ref[1:2, :] * z1 + cw_ref[2:3, :] * z
    post = p_ref[:, 2 * cw:3 * cw].astype(_F32)
    ya = jnp.dot((post * conv).astype(_BF16), wa_ref[...], preferred_element_type=_F32)

    c2 = 3 * cw
    v = _gelu(p_ref[:, c2 + sw:c2 + 2 * sw].astype(_F32))
    v_ref[...] = _layer_norm(v, lng_ref[...], lnb_ref[...]).astype(_BF16)
    pos_p = lax.broadcasted_iota(jnp.int32, (SGU_BLOCK, SGU_BLOCK), 0)
    pos_q = lax.broadcasted_iota(jnp.int32, (SGU_BLOCK, SGU_BLOCK), 1)
    causal = (pos_q // CHUNK) <= (pos_p // CHUNK)
    for g in range(n_groups):
        cs = slice(g * SGU_GROUP_CH, (g + 1) * SGU_GROUP_CH)
        wm = jnp.where(causal, ws_ref[g], 0.0).astype(_BF16)
        vcat = jnp.concatenate(
            [v_ref[n * SGU_BLOCK:(n + 1) * SGU_BLOCK, cs] for n in range(n_blk)], axis=1)
        mixed = jnp.dot(wm, vcat, preferred_element_type=_F32) + bst_ref[:, g:g + 1]
        for n in range(n_blk):
            rs = slice(n * SGU_BLOCK, (n + 1) * SGU_BLOCK)
            u = _gelu(p_ref[rs, c2 + g * SGU_GROUP_CH:c2 + (g + 1) * SGU_GROUP_CH].astype(_F32))
            gb_ref[rs, cs] = (u * mixed[:, n * SGU_BLOCK:(n + 1) * SGU_BLOCK]).astype(_BF16)
    yb = jnp.dot(gb_ref[...], wb_ref[...], preferred_element_type=_F32)

    c3 = c2 + 2 * sw
    ga = jax.nn.sigmoid(p_ref[:, c3:c3 + d].astype(_F32) + bg_ref[:, 0:d])
    gbr = jax.nn.sigmoid(p_ref[:, c3 + d:c3 + 2 * d].astype(_F32) + bg_ref[:, d:2 * d])
    m = (ga * ya + gbr * yb).astype(_BF16)
    mix = jnp.dot(m, wo_ref[...], preferred_element_type=_F32)
    x1 = _layer_norm(alpha * x_ref[...] + mix, l1g_ref[...], l1b_ref[...])
    x1_ref[...] = x1
    x1b = x1.astype(_BF16)
    x1p_ref[...] = _pack_bf16_halves(x1b)

    logits = jnp.dot(x1b, wr_ref[...], preferred_element_type=_F32) + br_ref[...]
    lane = lax.broadcasted_iota(jnp.int32, logits.shape, 1).astype(_F32)
    neg_inf = jnp.float32(-jnp.inf)
    vals = jnp.where(lane < n_exp, logits, neg_inf)
    top_i = jnp.zeros(logits.shape, _F32)
    top_e = jnp.zeros(logits.shape, _F32)
    v_max = None
    for k in range(TOP_K):
        vk = jnp.max(vals, axis=1, keepdims=True)
        ik = jnp.min(jnp.where(vals == vk, lane, float(LANES)), axis=1, keepdims=True)
        vals = jnp.where(lane == ik, neg_inf, vals)
        if k == 0:
            v_max = vk
        top_i = jnp.where(lane == k, ik, top_i)
        top_e = jnp.where(lane == k, jnp.exp(vk - v_max), top_e)
    ti_ref[...] = top_i.astype(jnp.int32)
    tg_ref[...] = top_e / jnp.sum(top_e, axis=1, keepdims=True)


def _mixer(p, x2d, conv_w, w_a, ln_g, ln_b, w_s, b_s_t, w_b, b_gate, w_o, l1g, l1b,
           w_r, b_r, side, *, seq, alpha, n_exp):
    t, d = x2d.shape
    nc = p.shape[1]
    cw = conv_w.shape[1]
    sw = ln_g.shape[1]
    tm = _tile(seq, MIXER_ROWS)
    assert tm % SGU_BLOCK == 0 and tm >= 8

    def const(shape):
        nd = len(shape)
        return pl.BlockSpec(shape, lambda i: (0,) * nd, pipeline_mode=pl.Buffered(1))

    n_side, side_rows = _side_cast_plan(side.shape[0], t // tm)
    side_in, side_out, side_shape = _side_cast_specs(side, n_side, side_rows, lambda i: i)
    weights_bytes = (cw * d + sw * d + d * d + d * LANES) * 2
    est = (2 * tm * nc * 2 + 4 * tm * d * 4 + weights_bytes + 2 * tm * sw * 2
           + 4 * tm * LANES * 4 + 2 * side_rows * side.shape[1] * 6)
    return pl.pallas_call(
        functools.partial(_mixer_kernel, tiles_per_seq=seq // tm, alpha=alpha, n_exp=n_exp,
                          n_side=n_side, n_steps=t // tm),
        grid=(t // tm,),
        in_specs=[
            pl.BlockSpec((tm, nc), lambda i: (i, 0)),
            pl.BlockSpec((tm, d), lambda i: (i, 0)),
            const(conv_w.shape), const(w_a.shape), const(ln_g.shape), const(ln_b.shape),
            const(w_s.shape), const(b_s_t.shape), const(w_b.shape), const(b_gate.shape),
            const(w_o.shape), const(l1g.shape), const(l1b.shape), const(w_r.shape),
            const(b_r.shape), side_in,
        ],
        out_specs=[
            pl.BlockSpec((tm, d), lambda i: (i, 0)),
            pl.BlockSpec((tm, d // 2), lambda i: (i, 0)),
            pl.BlockSpec((tm, LANES), lambda i: (i, 0)),
            pl.BlockSpec((tm, LANES), lambda i: (i, 0)),
            side_out,
        ],
        out_shape=[
            jax.ShapeDtypeStruct((t, d), _F32),
            jax.ShapeDtypeStruct((t, d // 2), jnp.uint32),
            jax.ShapeDtypeStruct((t, LANES), jnp.int32),
            jax.ShapeDtypeStruct((t, LANES), _F32),
            side_shape,
        ],
        scratch_shapes=[
            pltpu.VMEM((8, cw), _F32),
            pltpu.VMEM((tm, sw), _BF16),
            pltpu.VMEM((tm, sw), _BF16),
        ],
        compiler_params=pltpu.CompilerParams(
            dimension_semantics=("arbitrary",),
            vmem_limit_bytes=_vmem_limit(est + 24 * 1024 * 1024),
        ),
        name="mixer",
    )(p, x2d, conv_w, w_a, ln_g, ln_b, w_s, b_s_t, w_b, b_gate, w_o, l1g, l1b, w_r, b_r, side)


def _moe_kernel(be_ref, bv_ref,
                tokc_ref, tokn_ref, dstp_ref, dstc_ref, x_rows, x_hbm, wg_ref, wl_ref, bug_ref,
                bul_ref, wd_ref, bd_ref,
                y_hbm,
                xbuf, xb16, acc_ref, obuf, gsem, ssem, *, n_j):
    del be_ref
    i = pl.program_id(0)
    j = pl.program_id(1)
    n_blocks = pl.num_programs(0)
    slot = i % 2
    other = 1 - slot
    nv = bv_ref[i]
    n_groups = xbuf.shape[1]
    tmb, d = acc_ref.shape
    half = d // 2
    groups_per_step = n_groups // ISSUE_STEPS

    def gather_copy(tok, g, u, s):
        return pltpu.make_async_copy(
            x_rows.at[pl.ds(tok, 1)], xbuf.at[s, g, pl.ds(u, 1)], gsem.at[s])

    def scatter_copy(dst, g, u, s):
        return pltpu.make_async_copy(
            obuf.at[s, g, pl.ds(u, 1)],
            y_hbm.at[lax.shift_right_logical(dst, 3), pl.ds(lax.bitwise_and(dst, 7), 1)],
            ssem.at[s])

    def wait_gather(s):
        pltpu.make_async_copy(x_hbm.at[pl.ds(0, n_groups)], xbuf.at[s], gsem.at[s]).wait()

    def wait_scatter(s):
        pltpu.make_async_copy(obuf.at[s], y_hbm.at[pl.ds(0, n_groups)], ssem.at[s]).wait()

    def issue_rows(g0, n_g, tok_ref, dst_ref, s_gather, s_scatter):
        for gg in range(n_g):
            for u in range(SUBLANES):
                r = (g0 + gg) * SUBLANES + u
                if tok_ref is not None:
                    gather_copy(tok_ref[0, r], g0 + gg, u, s_gather).start()
                if dst_ref is not None:
                    scatter_copy(dst_ref[0, r], g0 + gg, u, s_scatter).start()

    def issue_block(tok_ref, dst_ref, s):
        def body(g, c):
            issue_rows(g, 1, tok_ref, dst_ref, s, s)
            return c
        lax.fori_loop(0, n_groups, body, 0)

    @pl.when(jnp.logical_and(j == 0, nv > 0))
    def _():
        @pl.when(i == 0)
        def _():
            obuf[...] = jnp.zeros_like(obuf)
            issue_block(tokc_ref, None, 0)

        @pl.when(i > 0)
        def _():
            wait_scatter(slot)

        wait_gather(slot)
        lo, hi = _unpack_bf16_halves(xbuf[slot].reshape(tmb, half))
        xb16[:, :half] = lo.astype(_BF16)
        xb16[:, half:] = hi.astype(_BF16)
        acc_ref[...] = jnp.zeros_like(acc_ref)

    @pl.when(nv > 0)
    def _():
        @pl.when(j < ISSUE_STEPS)
        def _():
            g0 = pl.multiple_of(j * groups_per_step, groups_per_step)
            issue_rows(g0, groups_per_step, tokn_ref, dstp_ref, other, other)

        xb = xb16[...]
        a_glu = jnp.dot(xb, wg_ref[...], preferred_element_type=_F32) + bug_ref[...]
        a_lin = jnp.dot(xb, wl_ref[...], preferred_element_type=_F32) + bul_ref[...]
        glu = jnp.minimum(a_glu, SWIGLU_LIMIT)
        lin = jnp.clip(a_lin, -SWIGLU_LIMIT, SWIGLU_LIMIT)
        h = glu * jax.nn.sigmoid(SWIGLU_ALPHA * glu) * (lin + 1.0)
        acc_ref[...] += jnp.dot(h.astype(_BF16), wd_ref[...], preferred_element_type=_F32)

        @pl.when(j == n_j - 1)
        def _():
            o = (acc_ref[...] + bd_ref[...]).astype(_BF16)
            obuf[slot] = _pack_bf16_halves(o).reshape(n_groups, SUBLANES, half)

            is_last = jnp.logical_or(
                i == n_blocks - 1, bv_ref[jnp.minimum(i + 1, n_blocks - 1)] == 0)

            @pl.when(is_last)
            def _():
                wait_gather(other)
                wait_scatter(other)
                issue_block(None, dstc_ref, slot)
                wait_scatter(slot)


def _moe(x1p, w_up, b_up, w_down, b_down, blk_expert, blk_valid, row_tok, row_dst):
    t, half = x1p.shape
    d = 2 * half
    n_exp, _, f2 = w_up.shape
    f = f2 // 2
    n_blocks = blk_expert.shape[0]
    tmb = row_dst.shape[-1]
    fc = _tile(f, MOE_FF_COLS)
    n_j = f // fc
    assert n_j >= 2

    def jj(i, j, bv):
        return jnp.where(bv[i] > 0, j, n_j - 1)

    assert tmb % (SUBLANES * ISSUE_STEPS) == 0 and t % SUBLANES == 0 and n_j >= ISSUE_STEPS
    n_groups = tmb // SUBLANES
    smem = functools.partial(pl.BlockSpec, memory_space=pltpu.SMEM)
    est = (4 * tmb * half * 4 + tmb * d * 2 + tmb * d * 4
           + 2 * (2 * d * fc + fc * d) * 2 + 4 * tmb * fc * 4 + 2 * tmb * d * 4)
    y_rows = pl.pallas_call(
        functools.partial(_moe_kernel, n_j=n_j),
        grid_spec=pltpu.PrefetchScalarGridSpec(
            num_scalar_prefetch=2,
            grid=(n_blocks, n_j),
            in_specs=[
                smem((None, 1, tmb), lambda i, j, be, bv: (i, 0, 0)),
                smem((None, 1, tmb), lambda i, j, be, bv: (i + 1, 0, 0)),
                smem((None, 1, tmb), lambda i, j, be, bv: (i, 0, 0)),
                smem((None, 1, tmb), lambda i, j, be, bv: (i + 1, 0, 0)),
                pl.BlockSpec(memory_space=pl.ANY),
                pl.BlockSpec(memory_space=pl.ANY),
                pl.BlockSpec((None, d, fc), lambda i, j, be, bv: (be[i], 0, jj(i, j, bv))),
                pl.BlockSpec((None, d, fc), lambda i, j, be, bv: (be[i], 0, n_j + jj(i, j, bv))),
                pl.BlockSpec((None, 1, fc), lambda i, j, be, bv: (be[i], 0, jj(i, j, bv))),
                pl.BlockSpec((None, 1, fc), lambda i, j, be, bv: (be[i], 0, n_j + jj(i, j, bv))),
                pl.BlockSpec((None, fc, d), lambda i, j, be, bv: (be[i], jj(i, j, bv), 0)),
                pl.BlockSpec((None, 1, d), lambda i, j, be, bv: (be[i], 0, 0)),
            ],
            out_specs=pl.BlockSpec(memory_space=pl.ANY),
            scratch_shapes=[
                pltpu.VMEM((2, n_groups, SUBLANES, half), jnp.uint32),
                pltpu.VMEM((tmb, d), _BF16),
                pltpu.VMEM((tmb, d), _F32),
                pltpu.VMEM((2, n_groups, SUBLANES, half), jnp.uint32),
                pltpu.SemaphoreType.DMA((2,)),
                pltpu.SemaphoreType.DMA((2,)),
            ],
        ),
        out_shape=jax.ShapeDtypeStruct(
            ((TOP_K * t + tmb) // SUBLANES, SUBLANES, half), jnp.uint32),
        compiler_params=pltpu.CompilerParams(
            dimension_semantics=("arbitrary", "arbitrary"),
            vmem_limit_bytes=_vmem_limit(est + 8 * 1024 * 1024),
        ),
        name="moe",
    )(blk_expert, blk_valid, row_tok, row_tok, row_dst, row_dst,
      x1p, x1p.reshape(t // SUBLANES, SUBLANES, half), w_up, w_up,
      b_up.reshape(n_exp, 1, f2), b_up.reshape(n_exp, 1, f2), w_down,
      b_down.reshape(n_exp, 1, d))
    return y_rows.reshape(TOP_K * t + tmb, half)


def _routing_tables(top_idx, n_exp, tmb):
    t = top_idx.shape[0]
    n_assign = t * TOP_K
    e_flat = top_idx.reshape(-1)
    order = jnp.argsort(e_flat).astype(jnp.int32)
    counts = jnp.bincount(e_flat, length=n_exp).astype(jnp.int32)
    padded = ((counts + tmb - 1) // tmb) * tmb
    starts = jnp.cumsum(counts) - counts
    pends = jnp.cumsum(padded)
    pstarts = pends - padded
    n_blocks = -(-n_assign // tmb) + n_exp
    blk_start = jnp.arange(n_blocks, dtype=jnp.int32) * tmb
    blk_expert = jnp.minimum(
        jnp.sum(pends[None, :] <= blk_start[:, None], axis=1), n_exp - 1).astype(jnp.int32)
    blk_valid = jnp.clip(counts[blk_expert] - (blk_start - pstarts[blk_expert]), 0, tmb)
    blk_valid = blk_valid.astype(jnp.int32)
    rows = jnp.arange(n_blocks * tmb, dtype=jnp.int32)
    row_e = jnp.repeat(blk_expert, tmb)
    rank = rows - pstarts[row_e]
    valid = rank < counts[row_e]
    assign = order[jnp.clip(starts[row_e] + rank, 0, n_assign - 1)]
    tok = assign // TOP_K
    slot_k = assign % TOP_K
    spare = TOP_K * t + rows % tmb
    row_tok = jnp.where(valid, tok, 0).astype(jnp.int32)
    row_dst = jnp.where(valid, slot_k * t + tok, spare).astype(jnp.int32)
    row_tok = jnp.concatenate([row_tok, jnp.zeros((tmb,), jnp.int32)])
    row_dst = jnp.concatenate([spare[:tmb], row_dst])
    return (blk_expert, blk_valid, row_tok.reshape(n_blocks + 1, 1, tmb),
            row_dst.reshape(n_blocks + 1, 1, tmb))


def _combine_kernel(*refs, alpha):
    y_refs = refs[:TOP_K]
    x1_ref, g_ref, l2g_ref, l2b_ref, o_ref = refs[TOP_K:]
    y_lo = y_hi = None
    for k in range(TOP_K):
        lo, hi = _unpack_bf16_halves(y_refs[k][...])
        g = g_ref[:, k:k + 1]
        y_lo = g * lo if k == 0 else y_lo + g * lo
        y_hi = g * hi if k == 0 else y_hi + g * hi
    y = jnp.concatenate([y_lo, y_hi], axis=1)
    o_ref[...] = _layer_norm(alpha * x1_ref[...] + y, l2g_ref[...], l2b_ref[...])


def _combine(y_rows, x1, gates, l2g, l2b, *, alpha):
    t, d = x1.shape
    tm = _tile(t, COMBINE_ROWS)
    tiles = t // tm
    est = 2 * (TOP_K // 2 + 2) * tm * d * 4 + 2 * tm * LANES * 4 + 6 * tm * d * 4
    return pl.pallas_call(
        functools.partial(_combine_kernel, alpha=alpha),
        grid=(tiles,),
        in_specs=[
            *[pl.BlockSpec((tm, d // 2),
                           functools.partial(lambda i, k: (k * tiles + i, 0), k=k))
              for k in range(TOP_K)],
            pl.BlockSpec((tm, d), lambda i: (i, 0)),
            pl.BlockSpec((tm, LANES), lambda i: (i, 0)),
            pl.BlockSpec((1, d), lambda i: (0, 0)),
            pl.BlockSpec((1, d), lambda i: (0, 0)),
        ],
        out_specs=pl.BlockSpec((tm, d), lambda i: (i, 0)),
        out_shape=jax.ShapeDtypeStruct((t, d), _F32),
        compiler_params=pltpu.CompilerParams(
            dimension_semantics=("arbitrary",),
            vmem_limit_bytes=_vmem_limit(est + 8 * 1024 * 1024),
        ),
        name="combine",
    )(*([y_rows] * TOP_K), x1, gates, l2g, l2b)


def kernel(x, w_in, conv_w, w_a_out, ln_v_g, ln_v_b, w_s, b_s, w_b_out, b_gate, w_o, ln1_g,
           ln1_b, w_router, b_router, w_up, b_up, w_down, b_down, ln2_g, ln2_b):
    bsz, seq, d = x.shape
    depth = w_in.shape[0]
    n_exp = w_router.shape[-1]
    alpha = (2.0 * depth) ** 0.25
    t = bsz * seq
    assert n_exp <= LANES and t * TOP_K % MOE_ROWS == 0
    h = x.reshape(t, d)
    for l in range(depth):
        p, w_up_bf = _inproj(h, w_in[l].astype(_BF16), w_up[l].reshape(-1, w_up.shape[-1]))
        w_r = jnp.pad(w_router[l], ((0, 0), (0, LANES - n_exp))).astype(_BF16)
        b_r = jnp.pad(b_router[l], (0, LANES - n_exp)).reshape(1, LANES)
        x1, x1p, top_i, top_g, w_down_bf = _mixer(
            p, h, conv_w[l], w_a_out[l].astype(_BF16), ln_v_g[l].reshape(1, -1),
            ln_v_b[l].reshape(1, -1), w_s[l], b_s[l].T, w_b_out[l].astype(_BF16),
            b_gate[l].reshape(1, -1), w_o[l].astype(_BF16), ln1_g[l].reshape(1, -1),
            ln1_b[l].reshape(1, -1), w_r, b_r, w_down[l].reshape(-1, w_down.shape[-1]),
            seq=seq, alpha=alpha, n_exp=n_exp)
        tables = _routing_tables(top_i[:, :TOP_K], n_exp, MOE_ROWS)
        y_rows = _moe(x1p, w_up_bf.reshape(w_up.shape[1:]), b_up[l],
                      w_down_bf.reshape(w_down.shape[1:]), b_down[l], *tables)
        h = _combine(y_rows, x1, top_g, ln2_g[l].reshape(1, -1),
                     ln2_b[l].reshape(1, -1), alpha=alpha)
    return h.reshape(bsz, seq, d)
```
